```python
import math
import jax, jax.numpy as jnp
from jax import lax
import numpy as np

D_MODEL = 1024
BATCH = 8
SEQ = 4096
DEPTH = 2

HG_HEADS = 4
HG_DK = 128
HG_DV = 128
HG_WIDTH = HG_HEADS * HG_DV
HG_CHUNK = 64
DA_HEADS = 4
DA_HEAD_DIM = 64
DA_V_DIM = 2 * DA_HEAD_DIM
DA_WIDTH = DA_HEADS * DA_V_DIM
Q_BLOCK = 128
D_FF = 2816
N_EXPERTS = 8
TOP_K = 2
D_FF_EXPERT = 3584
N_DENSE = (DEPTH + 1) // 2
N_MOE = DEPTH // 2
ADA_CHUNKS = 6
EPS = 1e-6
IN_SIZES = (HG_HEADS * HG_DK, HG_HEADS * HG_DK, HG_WIDTH, HG_WIDTH,
            DA_HEADS * 2 * DA_HEAD_DIM, DA_HEADS * 2 * DA_HEAD_DIM, DA_WIDTH,
            D_MODEL, D_MODEL)
D_IN = sum(IN_SIZES)

kernel_name = "hybrid_hgrn2_diffattn_moe_adaln"


def rmsnorm(x, g):
    xf = x.astype(jnp.float32)
    y = xf * lax.rsqrt(jnp.mean(xf * xf, axis=-1, keepdims=True) + EPS)
    return y * g.astype(jnp.float32)


def split_points():
    return [int(v) for v in np.cumsum(np.array(IN_SIZES))[:-1]]


def alibi_slopes(n_heads):
    return jnp.asarray(2.0 ** (-8.0 * np.arange(1, n_heads + 1) / n_heads), dtype=jnp.float32)


def hgrn_lower_bounds(lb_logits):
    p = jax.nn.softmax(lb_logits.astype(jnp.float32), axis=0)
    cum = jnp.cumsum(p, axis=0)
    return cum - cum[0:1]


def hgrn2_chunk_scan(q, k, v, log_f):
    B, H, S, dk = q.shape
    dv = v.shape[-1]
    L = HG_CHUNK
    nc = S // L

    def to_chunks(t):
        return jnp.moveaxis(t.reshape(B, H, nc, L, t.shape[-1]), 2, 0)

    causal = jnp.tril(jnp.ones((L, L), dtype=bool))

    def step(state, inp):
        qb, kb, vb, gb = inp
        b = jnp.cumsum(gb, axis=2)
        o_inter = jnp.einsum('bhtk,bhkv->bhtv', qb * jnp.exp(b), state)
        rel = b[:, :, :, None, :] - b[:, :, None, :, :]
        decay = jnp.exp(jnp.where(causal[:, :, None], rel, -jnp.inf))
        scores = jnp.einsum('bhtk,bhsk,bhtsk->bhts', qb, kb, decay)
        o_intra = jnp.einsum('bhts,bhsv->bhtv', scores, vb)
        b_last = b[:, :, -1:, :]
        k_dec = kb * jnp.exp(b_last - b)
        new_state = jnp.exp(b_last[:, :, 0, :])[..., None] * state + jnp.einsum('bhsk,bhsv->bhkv', k_dec, vb)
        return new_state, o_inter + o_intra

    s0 = jnp.zeros((B, H, dk, dv), jnp.float32)
    _, o = lax.scan(step, s0, (to_chunks(q), to_chunks(k), to_chunks(v), to_chunks(log_f)))
    return jnp.moveaxis(o, 0, 2).reshape(B, H, S, dv)


def diff_attention(q, k, v, lam):
    B, H, _, S, d = q.shape
    dv = v.shape[-1]
    nb = S // Q_BLOCK
    scale = d ** -0.5
    slopes = alibi_slopes(H)
    kpos = jnp.arange(S)
    q_blocks = jnp.moveaxis(q.reshape(B, H, 2, nb, Q_BLOCK, d), 3, 0)

    def block(args):
        qb, bi = args
        qpos = bi * Q_BLOCK + jnp.arange(Q_BLOCK)
        dist = qpos[:, None] - kpos[None, :]
        s = jnp.einsum('bhcqd,bhcsd->bhcqs', qb, k) * scale
        s = s - slopes[None, :, None, None, None] * dist.astype(jnp.float32)
        s = jnp.where(dist >= 0, s, -jnp.inf)
        p = jax.nn.softmax(s, axis=-1)
        w = p[:, :, 0] - lam * p[:, :, 1]
        return jnp.einsum('bhqs,bhsv->bhqv', w, v)

    o = lax.map(block, (q_blocks, jnp.arange(nb)))
    return jnp.moveaxis(o, 0, 2).reshape(B, H, S, dv)


def token_mixer(h, layer, w_in, lb, hg_norm_g, qn_g, kn_g, lam_vecs, subln_g, w_pa, w_pb, w_o):
    B, S, _ = h.shape
    dt = h.dtype
    proj = h @ w_in
    hq, hf, hi, hg, dq, dk, dvv, ga, gb = jnp.split(proj, split_points(), axis=-1)

    def heads(t, d):
        return t.reshape(B, S, -1, d).transpose(0, 2, 1, 3).astype(jnp.float32)
    z = heads(hf, HG_DK)
    lbh = lb.reshape(HG_HEADS, 1, HG_DK)
    log_f = jnp.logaddexp(jnp.log(lbh), jnp.log1p(-lbh) + jax.nn.log_sigmoid(z))
    k_in = (1.0 - lbh) * jax.nn.sigmoid(-z)
    o_hg = hgrn2_chunk_scan(heads(hq, HG_DK), k_in, heads(hi, HG_DV), log_f)
    o_hg = rmsnorm(o_hg, hg_norm_g).transpose(0, 2, 1, 3).reshape(B, S, HG_WIDTH)
    o_hg = (o_hg * jax.nn.silu(hg.astype(jnp.float32))).astype(dt)

    q = dq.reshape(B, S, DA_HEADS, 2, DA_HEAD_DIM).transpose(0, 2, 3, 1, 4)
    k = dk.reshape(B, S, DA_HEADS, 2, DA_HEAD_DIM).transpose(0, 2, 3, 1, 4)
    v = dvv.reshape(B, S, DA_HEADS, DA_V_DIM).transpose(0, 2, 1, 3).astype(jnp.float32)
    q = rmsnorm(q, qn_g)
    k = rmsnorm(k, kn_g)
    lam_init = 0.8 - 0.6 * math.exp(-0.3 * layer)
    lv = lam_vecs.astype(jnp.float32)
    lam = jnp.exp(jnp.sum(lv[0] * lv[1])) - jnp.exp(jnp.sum(lv[2] * lv[3])) + lam_init
    o_da = diff_attention(q, k, v, lam)
    o_da = (rmsnorm(o_da, subln_g) * (1.0 - lam_init)).transpose(0, 2, 1, 3).reshape(B, S, DA_WIDTH).astype(dt)

    y = jax.nn.sigmoid(ga) * (o_hg @ w_pa) + jax.nn.sigmoid(gb) * (o_da @ w_pb)
    return y @ w_o


def swiglu(h, w1, w3, w2):
    return (jax.nn.silu(h @ w1) * (h @ w3)) @ w2


def moe_swiglu(h, router, w1, w3, w2):
    B, S, D = h.shape
    t = h.reshape(B * S, D)
    logits = (t @ router).astype(jnp.float32)
    vals, idx = lax.top_k(logits, TOP_K)
    wts = jax.nn.softmax(vals, axis=-1)
    comb = jnp.sum(jax.nn.one_hot(idx, N_EXPERTS, dtype=jnp.float32) * wts[..., None], axis=1)
    out = jnp.zeros_like(t)
    for e in range(N_EXPERTS):
        out = out + comb[:, e:e + 1].astype(t.dtype) * swiglu(t, w1[e], w3[e], w2[e])
    return out.reshape(B, S, D)


def setup_inputs(seed: int = 0) -> dict:
    key = jax.random.key(seed)
    ks = jax.random.split(key, 24)
    f32 = jnp.float32
    D = D_MODEL

    def nrm(k, shape, scale):
        return jax.random.normal(k, shape, f32) * scale

    def gain(k, shape):
        return 1.0 + 0.05 * jax.random.normal(k, shape, f32)

    return {
        "x": nrm(ks[0], (BATCH, SEQ, D), 1.0),
        "c": nrm(ks[1], (BATCH, D), 1.0),
        "ada_w": nrm(ks[2], (DEPTH, D, ADA_CHUNKS * D), 0.5 * D ** -0.5),
        "ada_b": nrm(ks[3], (DEPTH, ADA_CHUNKS * D), 0.02),
        "norm_mix_g": gain(ks[4], (DEPTH, D)),
        "norm_ffn_g": gain(ks[5], (DEPTH, D)),
        "w_in": nrm(ks[6], (DEPTH, D, D_IN), D ** -0.5),
        "hgrn_lb_logits": nrm(ks[7], (DEPTH, HG_HEADS * HG_DK), 0.5),
        "hgrn_norm_g": gain(ks[8], (DEPTH, HG_DV)),
        "da_qnorm_g": gain(ks[9], (DEPTH, DA_HEAD_DIM)),
        "da_knorm_g": gain(ks[10], (DEPTH, DA_HEAD_DIM)),
        "da_lambda": nrm(ks[11], (DEPTH, 4, DA_HEAD_DIM), 0.1),
        "da_subln_g": gain(ks[12], (DEPTH, DA_V_DIM)),
        "w_branch_a": nrm(ks[13], (DEPTH, HG_WIDTH, D), HG_WIDTH ** -0.5),
        "w_branch_b": nrm(ks[14], (DEPTH, DA_WIDTH, D), DA_WIDTH ** -0.5),
        "w_out": nrm(ks[15], (DEPTH, D, D), D ** -0.5),
        "ffn_w1": nrm(ks[16], (N_DENSE, D, D_FF), D ** -0.5),
        "ffn_w3": nrm(ks[17], (N_DENSE, D, D_FF), D ** -0.5),
        "ffn_w2": nrm(ks[18], (N_DENSE, D_FF, D), D_FF ** -0.5),
        "moe_router": nrm(ks[19], (N_MOE, D, N_EXPERTS), D ** -0.5),
        "moe_w1": nrm(ks[20], (N_MOE, N_EXPERTS, D, D_FF_EXPERT), D ** -0.5),
        "moe_w3": nrm(ks[21], (N_MOE, N_EXPERTS, D, D_FF_EXPERT), D ** -0.5),
        "moe_w2": nrm(ks[22], (N_MOE, N_EXPERTS, D_FF_EXPERT, D), D_FF_EXPERT ** -0.5),
    }


def reference(x, c, ada_w, ada_b, norm_mix_g, norm_ffn_g, w_in, hgrn_lb_logits, hgrn_norm_g,
              da_qnorm_g, da_knorm_g, da_lambda, da_subln_g, w_branch_a, w_branch_b, w_out,
              ffn_w1, ffn_w3, ffn_w2, moe_router, moe_w1, moe_w3, moe_w2):
    dt = x.dtype
    lb_all = hgrn_lower_bounds(hgrn_lb_logits)
    cs = jax.nn.silu(c)
    for l in range(DEPTH):
        ada = cs @ ada_w[l] + ada_b[l]
        sh1, sc1, g1, sh2, sc2, g2 = jnp.split(ada[:, None, :], ADA_CHUNKS, axis=-1)
        h = (rmsnorm(x, norm_mix_g[l]) * (1.0 + sc1) + sh1).astype(dt)
        x = x + g1 * token_mixer(h, l, w_in[l], lb_all[l], hgrn_norm_g[l], da_qnorm_g[l], da_knorm_g[l],
                                 da_lambda[l], da_subln_g[l], w_branch_a[l], w_branch_b[l], w_out[l])
        h = (rmsnorm(x, norm_ffn_g[l]) * (1.0 + sc2) + sh2).astype(dt)
        if l % 2 == 0:
            f = swiglu(h, ffn_w1[l // 2], ffn_w3[l // 2], ffn_w2[l // 2])
        else:
            f = moe_swiglu(h, moe_router[l // 2], moe_w1[l // 2], moe_w3[l // 2], moe_w2[l // 2])
        x = x + g2 * f
    return x
```

```python
import functools
import math

import jax
import jax.numpy as jnp
from jax import lax
from jax.experimental import pallas as pl
from jax.experimental.pallas import tpu as pltpu

F32 = jnp.float32
BF16 = jnp.bfloat16
EPS = 1e-6

D_MODEL = 1024
HG_HEADS = 4
HG_D = 128
HG_WIDTH = HG_HEADS * HG_D
HG_CHUNK = 16
DA_HEADS = 4
DA_D = 64
DA_V = 2 * DA_D
DA_WIDTH = DA_HEADS * DA_V
N_EXPERTS = 8
ADA_CHUNKS = 6
LANES = 128
COL_HQ, COL_HF, COL_HI, COL_HG = 0, 512, 1024, 1536
COL_DQ, COL_DK, COL_DV = 2048, 2560, 3072
COL_GA, COL_GB = 3584, 4608
D_IN = 5632
NEG_BIG = -1e30
VMEM_LIMIT = 56 * 1024 * 1024


def _cparams(sem):
    return pltpu.CompilerParams(dimension_semantics=sem, vmem_limit_bytes=VMEM_LIMIT)


def _sigmoid(x):
    return 1.0 / (1.0 + jnp.exp(-x))


def _silu(x):
    return x * _sigmoid(x)


def _rms(x, gain):
    ms = jnp.mean(x * x, axis=-1, keepdims=True)
    return x * lax.rsqrt(ms + EPS) * gain


def _ada_kernel(c_ref, w_ref, b_ref, o_ref):
    cs = _silu(c_ref[...])
    o_ref[...] = jnp.dot(cs, w_ref[...], precision=lax.Precision.HIGHEST,
                         preferred_element_type=F32) + b_ref[...]


def _ada(c, ada_w, ada_b):
    depth, d, n6 = ada_w.shape
    b = c.shape[0]
    tn = 1536
    return pl.pallas_call(
        _ada_kernel,
        grid=(depth, n6 // tn),
        in_specs=[
            pl.BlockSpec((b, d), lambda l, j: (0, 0)),
            pl.BlockSpec((None, d, tn), lambda l, j: (l, 0, j)),
            pl.BlockSpec((None, 1, tn), lambda l, j: (l, 0, j)),
        ],
        out_specs=pl.BlockSpec((None, b, tn), lambda l, j: (l, 0, j)),
        out_shape=jax.ShapeDtypeStruct((depth, b, n6), F32),
        compiler_params=_cparams(("arbitrary", "arbitrary")),
        name="ada_mod",
    )(c, ada_w, ada_b.reshape(depth, 1, n6))


def _inproj_kernel(x_ref, mod_ref, g_ref, w_ref, o_ref, h_scr):
    @pl.when(pl.program_id(1) == 0)
    def _():
        h = _rms(x_ref[...], g_ref[...]) * (1.0 + mod_ref[1:2, :]) + mod_ref[0:1, :]
        h_scr[...] = h.astype(BF16)

    o_ref[...] = jnp.dot(h_scr[...], w_ref[...], preferred_element_type=F32).astype(o_ref.dtype)


def _inproj(x2, mod, gain, w_in_bf, seq):
    n, d = x2.shape
    tm = min(1024, seq)
    tn = 1408
    return pl.pallas_call(
        _inproj_kernel,
        grid=(n // tm, D_IN // tn),
        in_specs=[
            pl.BlockSpec((tm, d), lambda i, j: (i, 0)),
            pl.BlockSpec((None, ADA_CHUNKS, d), lambda i, j: (i * tm // seq, 0, 0)),
            pl.BlockSpec((1, d), lambda i, j: (0, 0)),
            pl.BlockSpec((d, tn), lambda i, j: (0, j)),
        ],
        out_specs=pl.BlockSpec((tm, tn), lambda i, j: (i, j)),
        out_shape=jax.ShapeDtypeStruct((n, D_IN), F32),
        scratch_shapes=[pltpu.VMEM((tm, d), BF16)],
        compiler_params=_cparams(("arbitrary", "arbitrary")),
        name="norm_inproj",
    )(x2, mod, gain.reshape(1, d), w_in_bf)


def _hgrn_kernel(q_ref, z_ref, i_ref, og_ref, lb_ref, g_ref, o_ref, st_scr, *, n_chunks):
    c_rows = HG_CHUNK

    @pl.when(pl.program_id(1) == 0)
    def _():
        st_scr[...] = jnp.zeros_like(st_scr)

    row = lax.broadcasted_iota(jnp.int32, (c_rows, HG_D), 0)
    gain = g_ref[...]
    lb = lb_ref[...]
    log_lb = jnp.log(lb)
    log1m_lb = jnp.log1p(-lb)
    one_m_lb = 1.0 - lb

    def body(c, carry):
        r0 = pl.multiple_of(c * c_rows, c_rows)
        for h in range(HG_HEADS):
            cs = slice(h * HG_D, (h + 1) * HG_D)
            z = z_ref[pl.ds(r0, c_rows), cs].astype(F32)
            q = q_ref[pl.ds(r0, c_rows), cs].astype(F32)
            v = i_ref[pl.ds(r0, c_rows), cs].astype(F32)
            og = og_ref[pl.ds(r0, c_rows), cs].astype(F32)

            e = jnp.exp(-jnp.abs(z))
            log_sig = jnp.minimum(z, 0.0) - jnp.log1p(e)
            y = log1m_lb[:, cs] + log_sig
            a = log_lb[:, cs]
            log_f = jnp.maximum(a, y) + jnp.log1p(jnp.exp(-jnp.abs(a - y)))
            sig_neg = jnp.where(z >= 0.0, e, 1.0) / (1.0 + e)
            k_in = one_m_lb[:, cs] * sig_neg

            b = log_f
            for d in (1, 2, 4, 8):
                b = b + jnp.where(row >= d, pltpu.roll(b, d, 0), 0.0)
            b_last = b[c_rows - 1:c_rows, :]

            st = st_scr[h]
            qe = (q * jnp.exp(b)).astype(BF16)
            o = lax.dot_general(qe, st.astype(BF16), (((1,), (1,)), ((), ())),
                                preferred_element_type=F32)
            for s in range(c_rows):
                rel = jnp.where(row >= s, b - b[s:s + 1, :], -jnp.inf)
                w = jnp.exp(rel) * q * k_in[s:s + 1, :]
                o = o + jnp.sum(w, axis=-1, keepdims=True) * v[s:s + 1, :]

            k_dec = k_in * jnp.exp(b_last - b)
            u = lax.dot_general(v.astype(BF16), k_dec.astype(BF16), (((0,), (0,)), ((), ())),
                                preferred_element_type=F32)
            st_scr[h] = st * jnp.exp(b_last) + u

            out = _rms(o, gain) * _silu(og)
            o_ref[pl.ds(r0, c_rows), cs] = out.astype(o_ref.dtype)
        return carry

    lax.fori_loop(0, n_chunks, body, 0)


def _hgrn(proj, lb, gain, batch, seq):
    n = proj.shape[0]
    ts = min(512, seq)
    nsb = seq // ts
    wb = HG_WIDTH

    def col(cb):
        return pl.BlockSpec((ts, wb), lambda b, i: (b * nsb + i, cb))

    return pl.pallas_call(
        functools.partial(_hgrn_kernel, n_chunks=ts // HG_CHUNK),
        grid=(batch, nsb),
        in_specs=[col(COL_HQ // wb), col(COL_HF // wb), col(COL_HI // wb), col(COL_HG // wb),
                  pl.BlockSpec((1, wb), lambda b, i: (0, 0)),
                  pl.BlockSpec((1, HG_D), lambda b, i: (0, 0))],
        out_specs=pl.BlockSpec((ts, wb), lambda b, i: (b * nsb + i, 0)),
        out_shape=jax.ShapeDtypeStruct((n, wb), BF16),
        scratch_shapes=[pltpu.VMEM((HG_HEADS, HG_D, HG_D), F32)],
        compiler_params=_cparams(("arbitrary", "arbitrary")),
        name="hgrn2_scan",
    )(proj, proj, proj, proj, lb.reshape(1, wb), gain.reshape(1, HG_D))


def _attn_kernel(slope_ref, q_ref, k_ref, v_ref, qg_ref, kg_ref, lam_ref, sg_ref, o_ref,
                 kn_scr, qz_scr, m_scr, l_scr, acc_scr, *, tq, tk, seq, lam_init):
    i = pl.program_id(2)
    lane = lax.broadcasted_iota(jnp.int32, (1, LANES), 1)
    lo = lane < DA_D

    def norm_halves(x, g):
        sq = x * x
        s_lo = jnp.sum(jnp.where(lo, sq, 0.0), axis=-1, keepdims=True)
        s_hi = jnp.sum(jnp.where(lo, 0.0, sq), axis=-1, keepdims=True)
        ms = jnp.where(lo, s_lo, s_hi) * (1.0 / DA_D)
        return x * lax.rsqrt(ms + EPS) * g

    @pl.when(i == 0)
    def _():
        def kb(j, c):
            r = pl.multiple_of(j * tk, tk)
            kn_scr[pl.ds(r, tk), :] = norm_halves(k_ref[pl.ds(r, tk), :].astype(F32), kg_ref[...]).astype(BF16)
            return c
        lax.fori_loop(0, seq // tk, kb, 0)

    qn = norm_halves(q_ref[...].astype(F32), qg_ref[...]) * (DA_D ** -0.5)
    qz_scr[0] = jnp.where(lo, qn, 0.0).astype(BF16)
    qz_scr[1] = jnp.where(lo, 0.0, qn).astype(BF16)
    m_scr[...] = jnp.full_like(m_scr, NEG_BIG)
    l_scr[...] = jnp.zeros_like(l_scr)
    acc_scr[...] = jnp.zeros_like(acc_scr)

    slope = slope_ref[...]
    colf = lax.broadcasted_iota(jnp.int32, (1, tk), 1).astype(F32)
    rowi = lax.broadcasted_iota(jnp.int32, (tq, tk), 0)
    coli = lax.broadcasted_iota(jnp.int32, (tq, tk), 1)

    def step(j, masked):
        r = pl.multiple_of(j * tk, tk)
        off = j * tk - i * tq
        kc = kn_scr[pl.ds(r, tk), :]
        vc = v_ref[pl.ds(r, tk), :].astype(BF16)
        cb = slope * (colf + off.astype(F32))
        for c in range(2):
            s = lax.dot_general(qz_scr[c], kc, (((1,), (1,)), ((), ())),
                                preferred_element_type=F32) + cb
            if masked:
                s = jnp.where(rowi >= coli + off, s, NEG_BIG)
            m_old = m_scr[c]
            m_new = jnp.maximum(m_old, jnp.max(s, axis=-1, keepdims=True))
            alpha = jnp.exp(m_old - m_new)
            p = jnp.exp(s - m_new)
            l_scr[c] = alpha * l_scr[c] + jnp.sum(p, axis=-1, keepdims=True)
            acc_scr[c] = alpha * acc_scr[c] + jnp.dot(p.astype(BF16), vc, preferred_element_type=F32)
            m_scr[c] = m_new

    ratio = tq // tk
    lax.fori_loop(0, i * ratio, lambda j, c: (step(j, False), c)[1], 0)
    lax.fori_loop(i * ratio, (i + 1) * ratio, lambda j, c: (step(j, True), c)[1], 0)

    lv = lam_ref[...]
    lam = (jnp.exp(jnp.sum(lv[0:1, :] * lv[1:2, :], axis=-1, keepdims=True))
           - jnp.exp(jnp.sum(lv[2:3, :] * lv[3:4, :], axis=-1, keepdims=True)) + lam_init)
    o = acc_scr[0] / l_scr[0] - lam * (acc_scr[1] / l_scr[1])
    o_ref[...] = (_rms(o, sg_ref[...]) * (1.0 - lam_init)).astype(o_ref.dtype)


def _attn(proj, qg, kg, lam_vecs, sg, layer, batch, seq):
    n = proj.shape[0]
    tq = tk = min(512, seq)
    nqb = seq // tq
    lam_init = 0.8 - 0.6 * math.exp(-0.3 * layer)
    slopes = jnp.asarray(2.0 ** (-8.0 * jnp.arange(1, DA_HEADS + 1) / DA_HEADS), F32)
    slopes = jnp.broadcast_to(slopes[:, None, None], (DA_HEADS, 1, tk))
    vec = lambda a: jnp.concatenate([a, a]).reshape(1, DA_V)
    return pl.pallas_call(
        functools.partial(_attn_kernel, tq=tq, tk=tk, seq=seq, lam_init=lam_init),
        grid=(batch, DA_HEADS, nqb),
        in_specs=[
            pl.BlockSpec((None, 1, tk), lambda b, h, i: (h, 0, 0)),
            pl.BlockSpec((tq, DA_V), lambda b, h, i: (b * nqb + i, COL_DQ // DA_V + h)),
            pl.BlockSpec((seq, DA_V), lambda b, h, i: (b, COL_DK // DA_V + h)),
            pl.BlockSpec((seq, DA_V), lambda b, h, i: (b, COL_DV // DA_V + h)),
            pl.BlockSpec((1, DA_V), lambda b, h, i: (0, 0)),
            pl.BlockSpec((1, DA_V), lambda b, h, i: (0, 0)),
            pl.BlockSpec((4, DA_D), lambda b, h, i: (0, 0)),
            pl.BlockSpec((1, DA_V), lambda b, h, i: (0, 0)),
        ],
        out_specs=pl.BlockSpec((tq, DA_V), lambda b, h, i: (b * nqb + i, h)),
        out_shape=jax.ShapeDtypeStruct((n, DA_WIDTH), BF16),
        scratch_shapes=[
            pltpu.VMEM((seq, DA_V), BF16),
            pltpu.VMEM((2, tq, DA_V), BF16),
            pltpu.VMEM((2, tq, 1), F32),
            pltpu.VMEM((2, tq, 1), F32),
            pltpu.VMEM((2, tq, DA_V), F32),
        ],
        compiler_params=_cparams(("arbitrary", "arbitrary", "arbitrary")),
        name="diff_attn",
    )(slopes, proj, proj, proj, vec(qg), vec(kg), lam_vecs, sg.reshape(1, DA_V))


def _merge_kernel(x_ref, ga0_ref, ga1_ref, gb0_ref, gb1_ref, ohg_ref, oda_ref, wpa_ref, wpb_ref, wo_ref,
                  mod_ref, gf_ref, xo_ref, h2_ref):
    a = jnp.dot(ohg_ref[...], wpa_ref[...], preferred_element_type=F32)
    b = jnp.dot(oda_ref[...], wpb_ref[...], preferred_element_type=F32)
    ga = jnp.concatenate([ga0_ref[...], ga1_ref[...]], axis=1).astype(F32)
    gb = jnp.concatenate([gb0_ref[...], gb1_ref[...]], axis=1).astype(F32)
    y = _sigmoid(ga) * a + _sigmoid(gb) * b
    mix = jnp.dot(y.astype(BF16), wo_ref[...], preferred_element_type=F32)
    xn = x_ref[...] + mod_ref[2:3, :] * mix
    xo_ref[...] = xn
    h = _rms(xn, gf_ref[...]) * (1.0 + mod_ref[4:5, :]) + mod_ref[3:4, :]
    h2_ref[...] = h.astype(h2_ref.dtype)


def _merge(x2, proj, o_hg, o_da, wpa, wpb, wo, mod, gain_ffn, seq, h2_dtype):
    n, d = x2.shape
    tm = min(512, seq)
    row = lambda w: pl.BlockSpec((tm, w), lambda i: (i, 0))
    full = lambda a: pl.BlockSpec(a.shape, lambda i: (0, 0))
    half = d // 2
    gate = lambda cb: pl.BlockSpec((tm, half), lambda i: (i, cb))
    return pl.pallas_call(
        _merge_kernel,
        grid=(n // tm,),
        in_specs=[
            row(d),
            gate(COL_GA // half), gate(COL_GA // half + 1),
            gate(COL_GB // half), gate(COL_GB // half + 1),
            row(HG_WIDTH), row(DA_WIDTH),
            full(wpa), full(wpb), full(wo),
            pl.BlockSpec((None, ADA_CHUNKS, d), lambda i: (i * tm // seq, 0, 0)),
            pl.BlockSpec((1, d), lambda i: (0, 0)),
        ],
        out_specs=[row(d), row(d)],
        out_shape=[jax.ShapeDtypeStruct((n, d), F32), jax.ShapeDtypeStruct((n, d), h2_dtype)],
        compiler_params=_cparams(("arbitrary",)),
        name="merge_outproj",
    )(x2, proj, proj, proj, proj, o_hg, o_da, wpa, wpb, wo, mod, gain_ffn.reshape(1, d))


def _ffn_kernel(h_ref, x_ref, mod_ref, w1_ref, w3_ref, w2_ref, o_ref, acc_scr):
    f = pl.program_id(1)

    @pl.when(f == 0)
    def _():
        acc_scr[...] = jnp.zeros_like(acc_scr)

    h = h_ref[...]
    g = _silu(jnp.dot(h, w1_ref[...], preferred_element_type=F32)) * jnp.dot(
        h, w3_ref[...], preferred_element_type=F32)
    acc_scr[...] += jnp.dot(g.astype(BF16), w2_ref[...], preferred_element_type=F32)

    @pl.when(f == pl.num_programs(1) - 1)
    def _():
        o_ref[...] = x_ref[...] + mod_ref[5:6, :] * acc_scr[...]


def _ffn(h2, x2, mod, w1, w3, w2, seq):
    n, d = x2.shape
    dff = w1.shape[1]
    tm = min(512, seq)
    tf = 1408
    return pl.pallas_call(
        _ffn_kernel,
        grid=(n // tm, dff // tf),
        in_specs=[
            pl.BlockSpec((tm, d), lambda i, f: (i, 0)),
            pl.BlockSpec((tm, d), lambda i, f: (i, 0)),
            pl.BlockSpec((None, ADA_CHUNKS, d), lambda i, f: (i * tm // seq, 0, 0)),
            pl.BlockSpec((d, tf), lambda i, f: (0, f)),
            pl.BlockSpec((d, tf), lambda i, f: (0, f)),
            pl.BlockSpec((tf, d), lambda i, f: (f, 0)),
        ],
        out_specs=pl.BlockSpec((tm, d), lambda i, f: (i, 0)),
        out_shape=jax.ShapeDtypeStruct((n, d), F32),
        scratch_shapes=[pltpu.VMEM((tm, d), F32)],
        compiler_params=_cparams(("arbitrary", "arbitrary")),
        name="dense_swiglu",
    )(h2, x2, mod, w1, w3, w2)


def _route_kernel(h_ref, r_ref, route_ref, cnt_ref, carry_scr):
    tm = h_ref.shape[0]

    @pl.when(pl.program_id(0) == 0)
    def _():
        carry_scr[...] = jnp.zeros_like(carry_scr)

    logits = jnp.dot(h_ref[...], r_ref[...], precision=lax.Precision.HIGHEST,
                     preferred_element_type=F32)
    lane = lax.broadcasted_iota(jnp.int32, (tm, LANES), 1).astype(F32)
    lg = jnp.where(lane < N_EXPERTS, logits, -jnp.inf)
    m0 = jnp.max(lg, axis=-1, keepdims=True)
    i0 = jnp.min(jnp.where(lg == m0, lane, float(LANES)), axis=-1, keepdims=True)
    lg1 = jnp.where(lane == i0, -jnp.inf, lg)
    m1 = jnp.max(lg1, axis=-1, keepdims=True)
    i1 = jnp.min(jnp.where(lg1 == m1, lane, float(LANES)), axis=-1, keepdims=True)
    e = jnp.exp(m1 - m0)
    w0 = 1.0 / (1.0 + e)
    w1 = e / (1.0 + e)

    sel = jnp.where(lane == i0, 1.0, jnp.where(lane == i1, 1.0, 0.0))
    rr = lax.broadcasted_iota(jnp.int32, (tm, tm), 0)
    cc = lax.broadcasted_iota(jnp.int32, (tm, tm), 1)
    tri = jnp.where(rr > cc, 1.0, 0.0).astype(BF16)
    before = jnp.dot(tri, sel.astype(BF16), preferred_element_type=F32) + carry_scr[...]
    r0 = jnp.sum(jnp.where(lane == i0, before, 0.0), axis=-1, keepdims=True)
    r1 = jnp.sum(jnp.where(lane == i1, before, 0.0), axis=-1, keepdims=True)
    carry_scr[...] += jnp.sum(sel, axis=0, keepdims=True)

    route = jnp.where(lane == 0.0, i0, jnp.where(lane == 1.0, i1, jnp.where(
        lane == 2.0, r0, jnp.where(lane == 3.0, r1, jnp.where(
            lane == 4.0, w0, jnp.where(lane == 5.0, w1, 0.0))))))
    route_ref[...] = route
    cnt_ref[...] = carry_scr[...]


def _route(h2, router):
    n, d = h2.shape
    tm = min(512, n)
    r_pad = jnp.zeros((d, LANES), F32).at[:, :N_EXPERTS].set(router)
    return pl.pallas_call(
        _route_kernel,
        grid=(n // tm,),
        in_specs=[pl.BlockSpec((tm, d), lambda i: (i, 0)),
                  pl.BlockSpec((d, LANES), lambda i: (0, 0))],
        out_specs=[pl.BlockSpec((tm, LANES), lambda i: (i, 0)),
                   pl.BlockSpec((1, LANES), lambda i: (0, 0))],
        out_shape=[jax.ShapeDtypeStruct((n, LANES), F32), jax.ShapeDtypeStruct((1, LANES), F32)],
        scratch_shapes=[pltpu.VMEM((1, LANES), F32)],
        compiler_params=_cparams(("arbitrary",)),
        name="moe_route",
    )(h2, r_pad)


def _row_copy(src_hbm, dst_ref, src_row, dst_row, sem):
    return pltpu.make_async_copy(src_hbm.at[pl.ds(src_row, 1)], dst_ref.at[pl.ds(dst_row, 1)], sem)


def _dispatch_kernel(pos_ref, h_hbm, xs_init_hbm, xs_hbm, sem, *, tg):
    del xs_init_hbm
    base = pl.program_id(0) * tg

    def issue(t, c):
        for k in range(2):
            _row_copy(h_hbm, xs_hbm, base + t, pos_ref[2 * t + k], sem).start()
        return c

    def drain(t, c):
        for k in range(2):
            _row_copy(h_hbm, xs_hbm, base + t, pos_ref[2 * t + k], sem).wait()
        return c

    lax.fori_loop(0, tg, issue, 0)
    lax.fori_loop(0, tg, drain, 0)


def _dispatch(h2, pos, n_rows):
    n, d = h2.shape
    tg = min(512, n)
    xs0 = jnp.zeros((n_rows, d), h2.dtype)
    return pl.pallas_call(
        functools.partial(_dispatch_kernel, tg=tg),
        grid=(n // tg,),
        in_specs=[pl.BlockSpec((2 * tg,), lambda i: (i,), memory_space=pltpu.SMEM),
                  pl.BlockSpec(memory_space=pl.ANY),
                  pl.BlockSpec(memory_space=pl.ANY)],
        out_specs=pl.BlockSpec(memory_space=pl.ANY),
        out_shape=jax.ShapeDtypeStruct((n_rows, d), h2.dtype),
        scratch_shapes=[pltpu.SemaphoreType.DMA(())],
        input_output_aliases={2: 0},
        compiler_params=_cparams(("arbitrary",)),
        name="moe_dispatch",
    )(pos, h2, xs0)


def _experts_kernel(te_ref, nu_ref, x_ref, w1_ref, w3_ref, w2_ref, y_ref, xb_scr, acc_scr):
    del te_ref
    i = pl.program_id(0)
    f = pl.program_id(1)
    used = i < nu_ref[0]

    @pl.when(jnp.logical_and(used, f == 0))
    def _():
        xb_scr[...] = x_ref[...].astype(BF16)
        acc_scr[...] = jnp.zeros_like(acc_scr)

    @pl.when(used)
    def _():
        xb = xb_scr[...]
        g = _silu(jnp.dot(xb, w1_ref[...], preferred_element_type=F32)) * jnp.dot(
            xb, w3_ref[...], preferred_element_type=F32)
        acc_scr[...] += jnp.dot(g.astype(BF16), w2_ref[...], preferred_element_type=F32)

    @pl.when(jnp.logical_and(used, f == pl.num_programs(1) - 1))
    def _():
        y_ref[...] = acc_scr[...]

    @pl.when(jnp.logical_and(jnp.logical_not(used), f == 0))
    def _():
        y_ref[...] = jnp.zeros_like(y_ref)


def _experts(xs, tile_expert, n_used, w1, w3, w2, tm):
    n_rows, d = xs.shape
    dff = w1.shape[2]
    tf = 896
    nf = dff // tf
    n_tiles = n_rows // tm

    def wmap(which):
        def index_map(i, f, te, nu):
            ii = jnp.minimum(i, nu[0] - 1)
            ff = jnp.where(i < nu[0], f, nf - 1)
            return (te[ii], 0, ff) if which == 0 else (te[ii], ff, 0)
        return index_map

    xmap = lambda i, f, te, nu: (jnp.minimum(i, nu[0] - 1), 0)
    return pl.pallas_call(
        _experts_kernel,
        grid_spec=pltpu.PrefetchScalarGridSpec(
            num_scalar_prefetch=2,
            grid=(n_tiles, nf),
            in_specs=[
                pl.BlockSpec((tm, d), xmap),
                pl.BlockSpec((None, d, tf), wmap(0)),
                pl.BlockSpec((None, d, tf), wmap(0)),
                pl.BlockSpec((None, tf, d), wmap(1)),
            ],
            out_specs=pl.BlockSpec((tm, d), lambda i, f, te, nu: (i, 0)),
            scratch_shapes=[pltpu.VMEM((tm, d), BF16), pltpu.VMEM((tm, d), F32)],
        ),
        out_shape=jax.ShapeDtypeStruct((n_rows, d), F32),
        compiler_params=_cparams(("arbitrary", "arbitrary")),
        name="moe_experts",
    )(tile_expert, n_used, xs, w1, w3, w2)


def _combine_kernel(pos_ref, y_hbm, x_ref, route_ref, mod_ref, o_ref, ybuf, sem, *, tc):
    def issue(t, c):
        for k in range(2):
            _row_copy(y_hbm, ybuf.at[k], pos_ref[2 * t + k], t, sem).start()
        return c

    def drain(t, c):
        for k in range(2):
            _row_copy(y_hbm, ybuf.at[k], pos_ref[2 * t + k], t, sem).wait()
        return c

    lax.fori_loop(0, tc, issue, 0)
    lax.fori_loop(0, tc, drain, 0)
    route = route_ref[...]
    f = route[:, 4:5] * ybuf[0] + route[:, 5:6] * ybuf[1]
    o_ref[...] = x_ref[...] + mod_ref[5:6, :] * f


def _combine(y, pos, x2, route, mod, seq):
    n, d = x2.shape
    tc = min(256, seq)
    return pl.pallas_call(
        functools.partial(_combine_kernel, tc=tc),
        grid=(n // tc,),
        in_specs=[
            pl.BlockSpec((2 * tc,), lambda i: (i,), memory_space=pltpu.SMEM),
            pl.BlockSpec(memory_space=pl.ANY),
            pl.BlockSpec((tc, d), lambda i: (i, 0)),
            pl.BlockSpec((tc, LANES), lambda i: (i, 0)),
            pl.BlockSpec((None, ADA_CHUNKS, d), lambda i: (i * tc // seq, 0, 0)),
        ],
        out_specs=pl.BlockSpec((tc, d), lambda i: (i, 0)),
        out_shape=jax.ShapeDtypeStruct((n, d), F32),
        scratch_shapes=[pltpu.VMEM((2, tc, d), F32), pltpu.SemaphoreType.DMA(())],
        compiler_params=_cparams(("arbitrary",)),
        name="moe_combine",
    )(pos, y, x2, route, mod)


def _moe(h2, x2, mod, router, w1, w3, w2, seq):
    n, _ = x2.shape
    tm = min(512, n)
    route, counts = _route(h2, router)
    cnt = counts[0, :N_EXPERTS].astype(jnp.int32)
    padded = (cnt + tm - 1) // tm * tm
    ends = jnp.cumsum(padded)
    offs = ends - padded
    ids = route[:, 0:2].astype(jnp.int32)
    ranks = route[:, 2:4].astype(jnp.int32)
    pos = (offs[ids] + ranks).reshape(-1)
    n_tiles = 2 * n // tm + N_EXPERTS
    starts = jnp.arange(n_tiles, dtype=jnp.int32) * tm
    tile_expert = jnp.minimum(jnp.sum(starts[:, None] >= ends[None, :], axis=1), N_EXPERTS - 1).astype(jnp.int32)
    n_used = (ends[-1:] // tm).astype(jnp.int32)

    xs = _dispatch(h2, pos, n_tiles * tm)
    y = _experts(xs, tile_expert, n_used, w1, w3, w2, tm)
    return _combine(y, pos, x2, route, mod, seq)


def kernel(x, c, ada_w, ada_b, norm_mix_g, norm_ffn_g, w_in, hgrn_lb_logits, hgrn_norm_g, da_qnorm_g, da_knorm_g, da_lambda, da_subln_g, w_branch_a, w_branch_b, w_out, ffn_w1, ffn_w3, ffn_w2, moe_router, moe_w1, moe_w3, moe_w2):
    batch, seq, d = x.shape
    depth = ada_w.shape[0]
    n = batch * seq
    x2 = x.reshape(n, d)

    p = jax.nn.softmax(hgrn_lb_logits.astype(F32), axis=0)
    cum = jnp.cumsum(p, axis=0)
    lb_all = cum - cum[0:1]

    mods = _ada(c, ada_w, ada_b).reshape(depth, batch, ADA_CHUNKS, d)

    for l in range(depth):
        mod = mods[l]
        moe_layer = l % 2 == 1
        proj = _inproj(x2, mod, norm_mix_g[l], w_in[l].astype(BF16), seq)
        o_hg = _hgrn(proj, lb_all[l], hgrn_norm_g[l], batch, seq)
        o_da = _attn(proj, da_qnorm_g[l], da_knorm_g[l], da_lambda[l], da_subln_g[l], l, batch, seq)
        x2, h2 = _merge(x2, proj, o_hg, o_da, w_branch_a[l].astype(BF16), w_branch_b[l].astype(BF16),
                        w_out[l].astype(BF16), mod, norm_ffn_g[l], seq, F32 if moe_layer else BF16)
        if moe_layer:
            x2 = _moe(h2, x2, mod, moe_router[l // 2], moe_w1[l // 2].astype(BF16),
                      moe_w3[l // 2].astype(BF16), moe_w2[l // 2].astype(BF16), seq)
        else:
            x2 = _ffn(h2, x2, mod, ffn_w1[l // 2].astype(BF16), ffn_w3[l // 2].astype(BF16),
                      ffn_w2[l // 2].astype(BF16), seq)
    return x2.reshape(batch, seq, d)
```

```python
import functools
import math

import jax
import jax.numpy as jnp
from jax import lax
from jax.experimental import pallas as pl
from jax.experimental.pallas import tpu as pltpu

F32 = jnp.float32
BF16 = jnp.bfloat16
EPS = 1e-6

D_MODEL = 1024
HG_HEADS = 4
HG_D = 128
HG_WIDTH = HG_HEADS * HG_D
HG_CHUNK = 16
DA_HEADS = 4
DA_D = 64
DA_V = 2 * DA_D
DA_WIDTH = DA_HEADS * DA_V
N_EXPERTS = 8
ADA_CHUNKS = 6
LANES = 128
COL_HQ, COL_HF, COL_HI, COL_HG = 0, 512, 1024, 1536
COL_DQ, COL_DK, COL_DV = 2048, 2560, 3072
COL_GA, COL_GB = 3584, 4608
D_IN = 5632
NEG_BIG = -1e30
LOG2E = 1.4426950408889634
VMEM_LIMIT = 56 * 1024 * 1024


def _cparams(sem):
    return pltpu.CompilerParams(dimension_semantics=sem, vmem_limit_bytes=VMEM_LIMIT)


def _sigmoid(x):
    return 1.0 / (1.0 + jnp.exp(-x))


def _silu(x):
    return x * _sigmoid(x)


def _rms(x, gain):
    ms = jnp.mean(x * x, axis=-1, keepdims=True)
    return x * lax.rsqrt(ms + EPS) * gain


def _ada_kernel(c_ref, w_ref, b_ref, o_ref):
    cs = _silu(c_ref[...])
    o_ref[...] = jnp.dot(cs, w_ref[...], precision=lax.Precision.HIGHEST,
                         preferred_element_type=F32) + b_ref[...]


def _ada(c, ada_w, ada_b):
    depth, d, n6 = ada_w.shape
    b = c.shape[0]
    tn = 1536
    return pl.pallas_call(
        _ada_kernel,
        grid=(depth, n6 // tn),
        in_specs=[
            pl.BlockSpec((b, d), lambda l, j: (0, 0)),
            pl.BlockSpec((None, d, tn), lambda l, j: (l, 0, j)),
            pl.BlockSpec((None, 1, tn), lambda l, j: (l, 0, j)),
        ],
        out_specs=pl.BlockSpec((None, b, tn), lambda l, j: (l, 0, j)),
        out_shape=jax.ShapeDtypeStruct((depth, b, n6), F32),
        compiler_params=_cparams(("arbitrary", "arbitrary")),
        name="ada_mod",
    )(c, ada_w, ada_b.reshape(depth, 1, n6))


def _inproj_kernel(x_ref, mod_ref, g_ref, w_ref, o_ref, h_scr):
    @pl.when(pl.program_id(1) == 0)
    def _():
        h = _rms(x_ref[...], g_ref[...]) * (1.0 + mod_ref[1:2, :]) + mod_ref[0:1, :]
        h_scr[...] = h.astype(BF16)

    o_ref[...] = jnp.dot(h_scr[...], w_ref[...], preferred_element_type=F32).astype(o_ref.dtype)


def _inproj(x2, mod, gain, w_in_bf, seq):
    n, d = x2.shape
    tm = min(512, seq)
    tn = D_IN // 2
    return pl.pallas_call(
        _inproj_kernel,
        grid=(n // tm, D_IN // tn),
        in_specs=[
            pl.BlockSpec((tm, d), lambda i, j: (i, 0)),
            pl.BlockSpec((None, ADA_CHUNKS, d), lambda i, j: (i * tm // seq, 0, 0)),
            pl.BlockSpec((1, d), lambda i, j: (0, 0)),
            pl.BlockSpec((d, tn), lambda i, j: (0, j)),
        ],
        out_specs=pl.BlockSpec((tm, tn), lambda i, j: (i, j)),
        out_shape=jax.ShapeDtypeStruct((n, D_IN), BF16),
        scratch_shapes=[pltpu.VMEM((tm, d), BF16)],
        compiler_params=_cparams(("arbitrary", "arbitrary")),
        name="norm_inproj",
    )(x2, mod, gain.reshape(1, d), w_in_bf)


def _hgrn_kernel(q_ref, z_ref, i_ref, og_ref, lb_ref, g_ref, o_ref, st_scr, *, n_chunks):
    c_rows = HG_CHUNK

    @pl.when(pl.program_id(1) == 0)
    def _():
        st_scr[...] = jnp.zeros_like(st_scr)

    row = lax.broadcasted_iota(jnp.int32, (c_rows, HG_D), 0)
    gain = g_ref[...]
    lb = lb_ref[...]
    log_lb = jnp.log(lb)
    log1m_lb = jnp.log1p(-lb)
    one_m_lb = 1.0 - lb

    def body(c, carry):
        r0 = pl.multiple_of(c * c_rows, c_rows)
        for h in range(HG_HEADS):
            cs = slice(h * HG_D, (h + 1) * HG_D)
            z = z_ref[pl.ds(r0, c_rows), cs].astype(F32)
            q = q_ref[pl.ds(r0, c_rows), cs].astype(F32)
            v = i_ref[pl.ds(r0, c_rows), cs].astype(F32)
            og = og_ref[pl.ds(r0, c_rows), cs].astype(F32)

            e = jnp.exp(-jnp.abs(z))
            log_sig = jnp.minimum(z, 0.0) - jnp.log1p(e)
            y = log1m_lb[:, cs] + log_sig
            a = log_lb[:, cs]
            log_f = jnp.maximum(a, y) + jnp.log1p(jnp.exp(-jnp.abs(a - y)))
            sig_neg = jnp.where(z >= 0.0, e, 1.0) / (1.0 + e)
            k_in = one_m_lb[:, cs] * sig_neg

            b = log_f
            for d in (1, 2, 4, 8):
                b = b + jnp.where(row >= d, pltpu.roll(b, d, 0), 0.0)
            b_last = b[c_rows - 1:c_rows, :]

            st = st_scr[h]
            qe = (q * jnp.exp(b)).astype(BF16)
            o = lax.dot_general(qe, st.astype(BF16), (((1,), (1,)), ((), ())),
                                preferred_element_type=F32)
            for s in range(c_rows):
                rel = jnp.where(row >= s, b - b[s:s + 1, :], -jnp.inf)
                w = jnp.exp(rel) * q * k_in[s:s + 1, :]
                o = o + jnp.sum(w, axis=-1, keepdims=True) * v[s:s + 1, :]

            k_dec = k_in * jnp.exp(b_last - b)
            u = lax.dot_general(v.astype(BF16), k_dec.astype(BF16), (((0,), (0,)), ((), ())),
                                preferred_element_type=F32)
            st_scr[h] = st * jnp.exp(b_last) + u

            out = _rms(o, gain) * _silu(og)
            o_ref[pl.ds(r0, c_rows), cs] = out.astype(o_ref.dtype)
        return carry

    lax.fori_loop(0, n_chunks, body, 0)


def _hgrn(proj, lb, gain, batch, seq):
    n = proj.shape[0]
    ts = min(512, seq)
    nsb = seq // ts
    wb = HG_WIDTH

    def col(cb):
        return pl.BlockSpec((ts, wb), lambda b, i: (b * nsb + i, cb))

    return pl.pallas_call(
        functools.partial(_hgrn_kernel, n_chunks=ts // HG_CHUNK),
        grid=(batch, nsb),
        in_specs=[col(COL_HQ // wb), col(COL_HF // wb), col(COL_HI // wb), col(COL_HG // wb),
                  pl.BlockSpec((1, wb), lambda b, i: (0, 0)),
                  pl.BlockSpec((1, HG_D), lambda b, i: (0, 0))],
        out_specs=pl.BlockSpec((ts, wb), lambda b, i: (b * nsb + i, 0)),
        out_shape=jax.ShapeDtypeStruct((n, wb), BF16),
        scratch_shapes=[pltpu.VMEM((HG_HEADS, HG_D, HG_D), F32)],
        compiler_params=_cparams(("arbitrary", "arbitrary")),
        name="hgrn2_scan",
    )(proj, proj, proj, proj, lb.reshape(1, wb), gain.reshape(1, HG_D))


def _attn_kernel(slope_ref, q_ref, k_ref, v_ref, qg_ref, kg_ref, lam_ref, sg_ref, o_ref,
                 kn_scr, vt_scr, bias_scr, qz_scr, m0_scr, m1_scr, l0_scr, l1_scr, acc0_scr, acc1_scr,
                 *, tq, tk, seq, lam_init):
    i = pl.program_id(2)
    lane = lax.broadcasted_iota(jnp.int32, (1, LANES), 1)
    lo = lane < DA_D
    slope2 = slope_ref[...] * LOG2E

    def norm_halves(x, g):
        sq = x * x
        s_lo = jnp.sum(jnp.where(lo, sq, 0.0), axis=-1, keepdims=True)
        s_hi = jnp.sum(jnp.where(lo, 0.0, sq), axis=-1, keepdims=True)
        ms = jnp.where(lo, s_lo, s_hi) * (1.0 / DA_D)
        return x * lax.rsqrt(ms + EPS) * g

    @pl.when(i == 0)
    def _():
        def kb(j, c):
            r = pl.multiple_of(j * tk, tk)
            kn_scr[j] = norm_halves(k_ref[pl.ds(r, tk), :].astype(F32), kg_ref[...]).astype(BF16)
            vt_scr[j] = v_ref[pl.ds(r, tk), :].astype(F32).T.astype(BF16)
            return c
        lax.fori_loop(0, seq // tk, kb, 0)
        key = lax.broadcasted_iota(jnp.int32, (tk, tq), 0).astype(F32)
        bias_scr[...] = key * slope2

    qn = norm_halves(q_ref[...].astype(F32), qg_ref[...]) * (DA_D ** -0.5 * LOG2E)
    qz_scr[0] = jnp.where(lo, qn, 0.0).astype(BF16)
    qz_scr[1] = jnp.where(lo, 0.0, qn).astype(BF16)
    stats = ((m0_scr, l0_scr, acc0_scr), (m1_scr, l1_scr, acc1_scr))
    for m_scr, l_scr, acc_scr in stats:
        m_scr[...] = jnp.full_like(m_scr, NEG_BIG)
        l_scr[...] = jnp.zeros_like(l_scr)
        acc_scr[...] = jnp.zeros_like(acc_scr)

    def steps(blocks):
        scores = []
        for j, masked in blocks:
            off = j * tk - i * tq
            kc = kn_scr[j]
            for c in range(2):
                s = lax.dot_general(kc, qz_scr[c], (((1,), (1,)), ((), ())),
                                    preferred_element_type=F32) + bias_scr[...]
                if masked:
                    keyi = lax.broadcasted_iota(jnp.int32, (tk, tq), 0)
                    qryi = lax.broadcasted_iota(jnp.int32, (tk, tq), 1)
                    s = jnp.where(keyi + off <= qryi, s, NEG_BIG)
                scores.append(s)
        for b, (j, masked) in enumerate(blocks):
            sigma = slope2 * (j * tk - i * tq).astype(F32)
            vt = vt_scr[j]
            for c in range(2):
                s = scores[2 * b + c]
                m_scr, l_scr, acc_scr = stats[c]
                m_old = m_scr[...]
                m_new = jnp.maximum(m_old, jnp.max(s, axis=0, keepdims=True) + sigma)
                alpha = jnp.exp2(m_old - m_new)
                p = jnp.exp2(s - (m_new - sigma))
                l_scr[...] = alpha * l_scr[...] + jnp.sum(p, axis=0, keepdims=True)
                acc_scr[...] = alpha * acc_scr[...] + jnp.dot(vt, p.astype(BF16), preferred_element_type=F32)
                m_scr[...] = m_new

    def pair(jj, carry):
        steps(((2 * jj, False), (2 * jj + 1, False)))
        return carry

    lax.fori_loop(0, i // 2, pair, 0)

    @pl.when(i % 2 == 1)
    def _():
        steps(((i - 1, False), (i, True)))

    @pl.when(i % 2 == 0)
    def _():
        steps(((i, True),))

    lv = lam_ref[...]
    lam = (jnp.exp(jnp.sum(lv[0:1, :] * lv[1:2, :], axis=-1, keepdims=True))
           - jnp.exp(jnp.sum(lv[2:3, :] * lv[3:4, :], axis=-1, keepdims=True)) + lam_init)
    o = acc0_scr[...] / l0_scr[...] - lam * (acc1_scr[...] / l1_scr[...])
    ms = jnp.mean(o * o, axis=0, keepdims=True)
    on = o * lax.rsqrt(ms + EPS) * sg_ref[...] * (1.0 - lam_init)
    o_ref[...] = on.T.astype(o_ref.dtype)


def _attn(proj, qg, kg, lam_vecs, sg, layer, batch, seq):
    n = proj.shape[0]
    tq = tk = min(512, seq)
    nqb = seq // tq
    lam_init = 0.8 - 0.6 * math.exp(-0.3 * layer)
    slopes = jnp.asarray(2.0 ** (-8.0 * jnp.arange(1, DA_HEADS + 1) / DA_HEADS), F32)
    slopes = jnp.broadcast_to(slopes[:, None, None], (DA_HEADS, 1, tq))
    vec = lambda a: jnp.concatenate([a, a]).reshape(1, DA_V)
    sg_cols = jnp.broadcast_to(sg.astype(F32)[:, None], (DA_V, tq))
    return pl.pallas_call(
        functools.partial(_attn_kernel, tq=tq, tk=tk, seq=seq, lam_init=lam_init),
        grid=(batch, DA_HEADS, nqb),
        in_specs=[
            pl.BlockSpec((None, 1, tq), lambda b, h, i: (h, 0, 0)),
            pl.BlockSpec((tq, DA_V), lambda b, h, i: (b * nqb + i, COL_DQ // DA_V + h)),
            pl.BlockSpec((seq, DA_V), lambda b, h, i: (b, COL_DK // DA_V + h)),
            pl.BlockSpec((seq, DA_V), lambda b, h, i: (b, COL_DV // DA_V + h)),
            pl.BlockSpec((1, DA_V), lambda b, h, i: (0, 0)),
            pl.BlockSpec((1, DA_V), lambda b, h, i: (0, 0)),
            pl.BlockSpec((4, DA_D), lambda b, h, i: (0, 0)),
            pl.BlockSpec((DA_V, tq), lambda b, h, i: (0, 0)),
        ],
        out_specs=pl.BlockSpec((tq, DA_V), lambda b, h, i: (b * nqb + i, h)),
        out_shape=jax.ShapeDtypeStruct((n, DA_WIDTH), BF16),
        scratch_shapes=[
            pltpu.VMEM((seq // tk, tk, DA_V), BF16),
            pltpu.VMEM((seq // tk, DA_V, tk), BF16),
            pltpu.VMEM((tk, tq), F32),
            pltpu.VMEM((2, tq, DA_V), BF16),
            pltpu.VMEM((1, tq), F32), pltpu.VMEM((1, tq), F32),
            pltpu.VMEM((1, tq), F32), pltpu.VMEM((1, tq), F32),
            pltpu.VMEM((DA_V, tq), F32), pltpu.VMEM((DA_V, tq), F32),
        ],
        compiler_params=_cparams(("arbitrary", "arbitrary", "arbitrary")),
        name="diff_attn",
    )(slopes, proj, proj, proj, vec(qg), vec(kg), lam_vecs, sg_cols)


def _merge_kernel(x_ref, ga0_ref, ga1_ref, gb0_ref, gb1_ref, ohg_ref, oda_ref, wpa_ref, wpb_ref, wo_ref,
                  mod_ref, gf_ref, xo_ref, h2_ref):
    a = jnp.dot(ohg_ref[...], wpa_ref[...], preferred_element_type=F32)
    b = jnp.dot(oda_ref[...], wpb_ref[...], preferred_element_type=F32)
    ga = jnp.concatenate([ga0_ref[...], ga1_ref[...]], axis=1).astype(F32)
    gb = jnp.concatenate([gb0_ref[...], gb1_ref[...]], axis=1).astype(F32)
    y = _sigmoid(ga) * a + _sigmoid(gb) * b
    mix = jnp.dot(y.astype(BF16), wo_ref[...], preferred_element_type=F32)
    xn = x_ref[...] + mod_ref[2:3, :] * mix
    xo_ref[...] = xn
    h = _rms(xn, gf_ref[...]) * (1.0 + mod_ref[4:5, :]) + mod_ref[3:4, :]
    h2_ref[...] = h.astype(h2_ref.dtype)


def _merge(x2, proj, o_hg, o_da, wpa, wpb, wo, mod, gain_ffn, seq, h2_dtype):
    n, d = x2.shape
    tm = min(512, seq)
    row = lambda w: pl.BlockSpec((tm, w), lambda i: (i, 0))
    full = lambda a: pl.BlockSpec(a.shape, lambda i: (0, 0))
    half = d // 2
    gate = lambda cb: pl.BlockSpec((tm, half), lambda i: (i, cb))
    return pl.pallas_call(
        _merge_kernel,
        grid=(n // tm,),
        in_specs=[
            row(d),
            gate(COL_GA // half), gate(COL_GA // half + 1),
            gate(COL_GB // half), gate(COL_GB // half + 1),
            row(HG_WIDTH), row(DA_WIDTH),
            full(wpa), full(wpb), full(wo),
            pl.BlockSpec((None, ADA_CHUNKS, d), lambda i: (i * tm // seq, 0, 0)),
            pl.BlockSpec((1, d), lambda i: (0, 0)),
        ],
        out_specs=[row(d), row(d)],
        out_shape=[jax.ShapeDtypeStruct((n, d), F32), jax.ShapeDtypeStruct((n, d), h2_dtype)],
        compiler_params=_cparams(("arbitrary",)),
        name="merge_outproj",
    )(x2, proj, proj, proj, proj, o_hg, o_da, wpa, wpb, wo, mod, gain_ffn.reshape(1, d))


def _ffn_kernel(h_ref, x_ref, mod_ref, w1_ref, w3_ref, w2_ref, o_ref, acc_scr):
    f = pl.program_id(1)

    @pl.when(f == 0)
    def _():
        acc_scr[...] = jnp.zeros_like(acc_scr)

    h = h_ref[...]
    g = _silu(jnp.dot(h, w1_ref[...], preferred_element_type=F32)) * jnp.dot(
        h, w3_ref[...], preferred_element_type=F32)
    acc_scr[...] += jnp.dot(g.astype(BF16), w2_ref[...], preferred_element_type=F32)

    @pl.when(f == pl.num_programs(1) - 1)
    def _():
        o_ref[...] = x_ref[...] + mod_ref[5:6, :] * acc_scr[...]


def _ffn(h2, x2, mod, w1, w3, w2, seq):
    n, d = x2.shape
    dff = w1.shape[1]
    tm = min(512, seq)
    tf = 1408
    return pl.pallas_call(
        _ffn_kernel,
        grid=(n // tm, dff // tf),
        in_specs=[
            pl.BlockSpec((tm, d), lambda i, f: (i, 0)),
            pl.BlockSpec((tm, d), lambda i, f: (i, 0)),
            pl.BlockSpec((None, ADA_CHUNKS, d), lambda i, f: (i * tm // seq, 0, 0)),
            pl.BlockSpec((d, tf), lambda i, f: (0, f)),
            pl.BlockSpec((d, tf), lambda i, f: (0, f)),
            pl.BlockSpec((tf, d), lambda i, f: (f, 0)),
        ],
        out_specs=pl.BlockSpec((tm, d), lambda i, f: (i, 0)),
        out_shape=jax.ShapeDtypeStruct((n, d), F32),
        scratch_shapes=[pltpu.VMEM((tm, d), F32)],
        compiler_params=_cparams(("arbitrary", "arbitrary")),
        name="dense_swiglu",
    )(h2, x2, mod, w1, w3, w2)


def _route_kernel(h_ref, r_ref, route_ref, cnt_ref, carry_scr):
    tm = h_ref.shape[0]

    @pl.when(pl.program_id(0) == 0)
    def _():
        carry_scr[...] = jnp.zeros_like(carry_scr)

    logits = jnp.dot(h_ref[...], r_ref[...], precision=lax.Precision.HIGHEST,
                     preferred_element_type=F32)
    lane = lax.broadcasted_iota(jnp.int32, (tm, LANES), 1).astype(F32)
    lg = jnp.where(lane < N_EXPERTS, logits, -jnp.inf)
    m0 = jnp.max(lg, axis=-1, keepdims=True)
    i0 = jnp.min(jnp.where(lg == m0, lane, float(LANES)), axis=-1, keepdims=True)
    lg1 = jnp.where(lane == i0, -jnp.inf, lg)
    m1 = jnp.max(lg1, axis=-1, keepdims=True)
    i1 = jnp.min(jnp.where(lg1 == m1, lane, float(LANES)), axis=-1, keepdims=True)
    e = jnp.exp(m1 - m0)
    w0 = 1.0 / (1.0 + e)
    w1 = e / (1.0 + e)

    sel = jnp.where(lane == i0, 1.0, jnp.where(lane == i1, 1.0, 0.0))
    rr = lax.broadcasted_iota(jnp.int32, (tm, tm), 0)
    cc = lax.broadcasted_iota(jnp.int32, (tm, tm), 1)
    tri = jnp.where(rr > cc, 1.0, 0.0).astype(BF16)
    before = jnp.dot(tri, sel.astype(BF16), preferred_element_type=F32) + carry_scr[...]
    r0 = jnp.sum(jnp.where(lane == i0, before, 0.0), axis=-1, keepdims=True)
    r1 = jnp.sum(jnp.where(lane == i1, before, 0.0), axis=-1, keepdims=True)
    carry_scr[...] += jnp.sum(sel, axis=0, keepdims=True)

    route = jnp.where(lane == 0.0, i0, jnp.where(lane == 1.0, i1, jnp.where(
        lane == 2.0, r0, jnp.where(lane == 3.0, r1, jnp.where(
            lane == 4.0, w0, jnp.where(lane == 5.0, w1, 0.0))))))
    route_ref[...] = route
    cnt_ref[...] = carry_scr[...]


def _route(h2, router):
    n, d = h2.shape
    tm = min(512, n)
    r_pad = jnp.zeros((d, LANES), F32).at[:, :N_EXPERTS].set(router)
    return pl.pallas_call(
        _route_kernel,
        grid=(n // tm,),
        in_specs=[pl.BlockSpec((tm, d), lambda i: (i, 0)),
                  pl.BlockSpec((d, LANES), lambda i: (0, 0))],
        out_specs=[pl.BlockSpec((tm, LANES), lambda i: (i, 0)),
                   pl.BlockSpec((1, LANES), lambda i: (0, 0))],
        out_shape=[jax.ShapeDtypeStruct((n, LANES), F32), jax.ShapeDtypeStruct((1, LANES), F32)],
        scratch_shapes=[pltpu.VMEM((1, LANES), F32)],
        compiler_params=_cparams(("arbitrary",)),
        name="moe_route",
    )(h2, r_pad)


def _row_copy(src_hbm, dst_ref, src_row, dst_row, sem):
    return pltpu.make_async_copy(src_hbm.at[pl.ds(src_row, 1)], dst_ref.at[pl.ds(dst_row, 1)], sem)


def _dispatch_kernel(pos_ref, h_ref, xs_init_hbm, xs_hbm, sem, *, tg):
    del xs_init_hbm

    def issue(t, c):
        for k in range(2):
            _row_copy(h_ref, xs_hbm, t, pos_ref[2 * t + k], sem).start()
        return c

    def drain(t, c):
        for k in range(2):
            _row_copy(h_ref, xs_hbm, t, pos_ref[2 * t + k], sem).wait()
        return c

    lax.fori_loop(0, tg, issue, 0)
    lax.fori_loop(0, tg, drain, 0)


def _dispatch(h2, pos, n_rows):
    n, d = h2.shape
    tg = min(512, n)
    xs0 = jnp.zeros((n_rows, d), h2.dtype)
    return pl.pallas_call(
        functools.partial(_dispatch_kernel, tg=tg),
        grid=(n // tg,),
        in_specs=[pl.BlockSpec((2 * tg,), lambda i: (i,), memory_space=pltpu.SMEM),
                  pl.BlockSpec((tg, d), lambda i: (i, 0)),
                  pl.BlockSpec(memory_space=pl.ANY)],
        out_specs=pl.BlockSpec(memory_space=pl.ANY),
        out_shape=jax.ShapeDtypeStruct((n_rows, d), h2.dtype),
        scratch_shapes=[pltpu.SemaphoreType.DMA(())],
        input_output_aliases={2: 0},
        compiler_params=_cparams(("arbitrary",)),
        name="moe_dispatch",
    )(pos, h2, xs0)


def _experts_kernel(te_ref, nu_ref, x_ref, w1_ref, w3_ref, w2_ref, y_ref, xb_scr, acc_scr):
    del te_ref
    i = pl.program_id(0)
    f = pl.program_id(1)
    used = i < nu_ref[0]

    @pl.when(jnp.logical_and(used, f == 0))
    def _():
        xb_scr[...] = x_ref[...].astype(BF16)
        acc_scr[...] = jnp.zeros_like(acc_scr)

    @pl.when(used)
    def _():
        xb = xb_scr[...]
        g = _silu(jnp.dot(xb, w1_ref[...], preferred_element_type=F32)) * jnp.dot(
            xb, w3_ref[...], preferred_element_type=F32)
        acc_scr[...] += jnp.dot(g.astype(BF16), w2_ref[...], preferred_element_type=F32)

    @pl.when(jnp.logical_and(used, f == pl.num_programs(1) - 1))
    def _():
        y_ref[...] = acc_scr[...]

    @pl.when(jnp.logical_and(jnp.logical_not(used), f == 0))
    def _():
        y_ref[...] = jnp.zeros_like(y_ref)


def _experts(xs, tile_expert, n_used, w1, w3, w2, tm):
    n_rows, d = xs.shape
    dff = w1.shape[2]
    tf = dff // 2
    nf = dff // tf
    n_tiles = n_rows // tm

    def wmap(which):
        def index_map(i, f, te, nu):
            ii = jnp.minimum(i, nu[0] - 1)
            ff = jnp.where(i < nu[0], f, nf - 1)
            return (te[ii], 0, ff) if which == 0 else (te[ii], ff, 0)
        return index_map

    xmap = lambda i, f, te, nu: (jnp.minimum(i, nu[0] - 1), 0)
    return pl.pallas_call(
        _experts_kernel,
        grid_spec=pltpu.PrefetchScalarGridSpec(
            num_scalar_prefetch=2,
            grid=(n_tiles, nf),
            in_specs=[
                pl.BlockSpec((tm, d), xmap),
                pl.BlockSpec((None, d, tf), wmap(0)),
                pl.BlockSpec((None, d, tf), wmap(0)),
                pl.BlockSpec((None, tf, d), wmap(1)),
            ],
            out_specs=pl.BlockSpec((tm, d), lambda i, f, te, nu: (i, 0)),
            scratch_shapes=[pltpu.VMEM((tm, d), BF16), pltpu.VMEM((tm, d), F32)],
        ),
        out_shape=jax.ShapeDtypeStruct((n_rows, d), F32),
        compiler_params=_cparams(("arbitrary", "arbitrary")),
        name="moe_experts",
    )(tile_expert, n_used, xs, w1, w3, w2)


def _combine_kernel(pos_ref, y_hbm, x_ref, route_ref, mod_ref, o_ref, ybuf, sem, *, tc):
    def issue(t, c):
        for k in range(2):
            _row_copy(y_hbm, ybuf.at[k], pos_ref[2 * t + k], t, sem).start()
        return c

    def drain(t, c):
        for k in range(2):
            _row_copy(y_hbm, ybuf.at[k], pos_ref[2 * t + k], t, sem).wait()
        return c

    lax.fori_loop(0, tc, issue, 0)
    lax.fori_loop(0, tc, drain, 0)
    route = route_ref[...]
    f = route[:, 4:5] * ybuf[0] + route[:, 5:6] * ybuf[1]
    o_ref[...] = x_ref[...] + mod_ref[5:6, :] * f


def _combine(y, pos, x2, route, mod, seq):
    n, d = x2.shape
    tc = min(256, seq)
    return pl.pallas_call(
        functools.partial(_combine_kernel, tc=tc),
        grid=(n // tc,),
        in_specs=[
            pl.BlockSpec((2 * tc,), lambda i: (i,), memory_space=pltpu.SMEM),
            pl.BlockSpec(memory_space=pl.ANY),
            pl.BlockSpec((tc, d), lambda i: (i, 0)),
            pl.BlockSpec((tc, LANES), lambda i: (i, 0)),
            pl.BlockSpec((None, ADA_CHUNKS, d), lambda i: (i * tc // seq, 0, 0)),
        ],
        out_specs=pl.BlockSpec((tc, d), lambda i: (i, 0)),
        out_shape=jax.ShapeDtypeStruct((n, d), F32),
        scratch_shapes=[pltpu.VMEM((2, tc, d), F32), pltpu.SemaphoreType.DMA(())],
        compiler_params=_cparams(("arbitrary",)),
        name="moe_combine",
    )(pos, y, x2, route, mod)


def _moe(h2, x2, mod, router, w1, w3, w2, seq):
    n, _ = x2.shape
    tm = min(512, n)
    route, counts = _route(h2, router)
    cnt = counts[0, :N_EXPERTS].astype(jnp.int32)
    padded = (cnt + tm - 1) // tm * tm
    ends = jnp.cumsum(padded)
    offs = ends - padded
    ids = route[:, 0:2].astype(jnp.int32)
    ranks = route[:, 2:4].astype(jnp.int32)
    pos = (offs[ids] + ranks).reshape(-1)
    n_tiles = 2 * n // tm + N_EXPERTS
    starts = jnp.arange(n_tiles, dtype=jnp.int32) * tm
    tile_expert = jnp.minimum(jnp.sum(starts[:, None] >= ends[None, :], axis=1), N_EXPERTS - 1).astype(jnp.int32)
    n_used = (ends[-1:] // tm).astype(jnp.int32)

    xs = _dispatch(h2, pos, n_tiles * tm)
    y = _experts(xs, tile_expert, n_used, w1, w3, w2, tm)
    return _combine(y, pos, x2, route, mod, seq)


def kernel(x, c, ada_w, ada_b, norm_mix_g, norm_ffn_g, w_in, hgrn_lb_logits, hgrn_norm_g, da_qnorm_g, da_knorm_g, da_lambda, da_subln_g, w_branch_a, w_branch_b, w_out, ffn_w1, ffn_w3, ffn_w2, moe_router, moe_w1, moe_w3, moe_w2):
    batch, seq, d = x.shape
    depth = ada_w.shape[0]
    n = batch * seq
    x2 = x.reshape(n, d)

    p = jax.nn.softmax(hgrn_lb_logits.astype(F32), axis=0)
    cum = jnp.cumsum(p, axis=0)
    lb_all = cum - cum[0:1]

    mods = _ada(c, ada_w, ada_b).reshape(depth, batch, ADA_CHUNKS, d)

    for l in range(depth):
        mod = mods[l]
        moe_layer = l % 2 == 1
        proj = _inproj(x2, mod, norm_mix_g[l], w_in[l].astype(BF16), seq)
        o_hg = _hgrn(proj, lb_all[l], hgrn_norm_g[l], batch, seq)
        o_da = _attn(proj, da_qnorm_g[l], da_knorm_g[l], da_lambda[l], da_subln_g[l], l, batch, seq)
        x2, h2 = _merge(x2, proj, o_hg, o_da, w_branch_a[l].astype(BF16), w_branch_b[l].astype(BF16),
                        w_out[l].astype(BF16), mod, norm_ffn_g[l], seq, F32 if moe_layer else BF16)
        if moe_layer:
            x2 = _moe(h2, x2, mod, moe_router[l // 2], moe_w1[l // 2].astype(BF16),
                      moe_w3[l // 2].astype(BF16), moe_w2[l // 2].astype(BF16), seq)
        else:
            x2 = _ffn(h2, x2, mod, ffn_w1[l // 2].astype(BF16), ffn_w3[l // 2].astype(BF16),
                      ffn_w2[l // 2].astype(BF16), seq)
    return x2.reshape(batch, seq, d)
```

```python
import functools
import math

import jax
import jax.numpy as jnp
from jax import lax
from jax.experimental import pallas as pl
from jax.experimental.pallas import tpu as pltpu

F32 = jnp.float32
BF16 = jnp.bfloat16
EPS = 1e-6

D_MODEL = 1024
HG_HEADS = 4
HG_D = 128
HG_WIDTH = HG_HEADS * HG_D
HG_CHUNK = 16
DA_HEADS = 4
DA_D = 64
DA_V = 2 * DA_D
DA_WIDTH = DA_HEADS * DA_V
N_EXPERTS = 8
ADA_CHUNKS = 6
LANES = 128
COL_HQ, COL_HF, COL_HI, COL_HG = 0, 512, 1024, 1536
COL_DQ, COL_DK, COL_DV = 2048, 2560, 3072
COL_GA, COL_GB = 3584, 4608
D_IN = 5632
NEG_BIG = -1e30
LOG2E = 1.4426950408889634
AUG_COLS = 12
V_PAD = 16
ROW_DMA_UNROLL = 8
BOUND_LIMIT = 80.0
VMEM_LIMIT = 56 * 1024 * 1024


def _cparams(sem):
    return pltpu.CompilerParams(dimension_semantics=sem, vmem_limit_bytes=VMEM_LIMIT)


def _sigmoid(x):
    return 1.0 / (1.0 + jnp.exp(-x))


def _silu(x):
    return x * _sigmoid(x)


def _rms(x, gain):
    ms = jnp.mean(x * x, axis=-1, keepdims=True)
    return x * lax.rsqrt(ms + EPS) * gain


def _ada_kernel(c_ref, w_ref, b_ref, o_ref):
    cs = _silu(c_ref[...])
    o_ref[...] = jnp.dot(cs, w_ref[...], precision=lax.Precision.HIGHEST,
                         preferred_element_type=F32) + b_ref[...]


def _ada(c, ada_w, ada_b):
    depth, d, n6 = ada_w.shape
    b = c.shape[0]
    tn = 1536
    return pl.pallas_call(
        _ada_kernel,
        grid=(depth, n6 // tn),
        in_specs=[
            pl.BlockSpec((b, d), lambda l, j: (0, 0)),
            pl.BlockSpec((None, d, tn), lambda l, j: (l, 0, j)),
            pl.BlockSpec((None, 1, tn), lambda l, j: (l, 0, j)),
        ],
        out_specs=pl.BlockSpec((None, b, tn), lambda l, j: (l, 0, j)),
        out_shape=jax.ShapeDtypeStruct((depth, b, n6), F32),
        compiler_params=_cparams(("arbitrary", "arbitrary")),
        name="ada_mod",
    )(c, ada_w, ada_b.reshape(depth, 1, n6))


def _inproj_kernel(x_ref, mod_ref, g_ref, w_ref, o_ref, h_scr):
    @pl.when(pl.program_id(1) == 0)
    def _():
        h = _rms(x_ref[...], g_ref[...]) * (1.0 + mod_ref[1:2, :]) + mod_ref[0:1, :]
        h_scr[...] = h.astype(BF16)

    o_ref[...] = jnp.dot(h_scr[...], w_ref[...], preferred_element_type=F32).astype(o_ref.dtype)


def _inproj(x2, mod, gain, w_in_bf, seq):
    n, d = x2.shape
    tm = min(512, seq)
    tn = D_IN // 2
    return pl.pallas_call(
        _inproj_kernel,
        grid=(n // tm, D_IN // tn),
        in_specs=[
            pl.BlockSpec((tm, d), lambda i, j: (i, 0)),
            pl.BlockSpec((None, ADA_CHUNKS, d), lambda i, j: (i * tm // seq, 0, 0)),
            pl.BlockSpec((1, d), lambda i, j: (0, 0)),
            pl.BlockSpec((d, tn), lambda i, j: (0, j)),
        ],
        out_specs=pl.BlockSpec((tm, tn), lambda i, j: (i, j)),
        out_shape=jax.ShapeDtypeStruct((n, D_IN), BF16),
        scratch_shapes=[pltpu.VMEM((tm, d), BF16)],
        compiler_params=_cparams(("arbitrary", "arbitrary")),
        name="norm_inproj",
    )(x2, mod, gain.reshape(1, d), w_in_bf)


def _hgrn_kernel(q_ref, z_ref, i_ref, og_ref, lb_ref, g_ref, o_ref, st_scr, *, n_chunks):
    c_rows = HG_CHUNK

    @pl.when(pl.program_id(1) == 0)
    def _():
        st_scr[...] = jnp.zeros_like(st_scr)

    half = c_rows // 2
    row = lax.broadcasted_iota(jnp.int32, (c_rows, HG_D), 0)
    row_half = lax.broadcasted_iota(jnp.int32, (half, HG_D), 0)
    gain = g_ref[...]
    lb = lb_ref[...]
    log_lb = jnp.log(lb)
    log1m_lb = jnp.log1p(-lb)
    one_m_lb = 1.0 - lb

    def body(c, carry):
        r0 = pl.multiple_of(c * c_rows, c_rows)
        for h in range(HG_HEADS):
            cs = slice(h * HG_D, (h + 1) * HG_D)
            z = z_ref[pl.ds(r0, c_rows), cs].astype(F32)
            q = q_ref[pl.ds(r0, c_rows), cs].astype(F32)
            v = i_ref[pl.ds(r0, c_rows), cs].astype(F32)
            og = og_ref[pl.ds(r0, c_rows), cs].astype(F32)

            e = jnp.exp(-jnp.abs(z))
            log_sig = jnp.minimum(z, 0.0) - jnp.log(1.0 + e)
            y = log1m_lb[:, cs] + log_sig
            a = log_lb[:, cs]
            log_f = jnp.maximum(a, y) + jnp.log(1.0 + jnp.exp(-jnp.abs(a - y)))
            sig_neg = jnp.where(z >= 0.0, e, 1.0) / (1.0 + e)
            k_in = one_m_lb[:, cs] * sig_neg

            b = log_f
            for d in (1, 2, 4, 8):
                b = b + jnp.where(row >= d, pltpu.roll(b, d, 0), 0.0)
            b_last = b[c_rows - 1:c_rows, :]

            st = st_scr[h]
            qe = (q * jnp.exp(b)).astype(BF16)
            o = lax.dot_general(qe, st.astype(BF16), (((1,), (1,)), ((), ())),
                                preferred_element_type=F32)
            o_half = [o[0:half, :], o[half:c_rows, :]]
            b_mid = b[half - 1:half, :]
            q_bot = q[half:c_rows, :] * jnp.exp(b[half:c_rows, :] - b_mid)
            k_top = k_in[0:half, :] * jnp.exp(b_mid - b[0:half, :])
            for s in range(c_rows):
                hi = s // half
                rel = jnp.where(row_half >= s - hi * half, b[hi * half:(hi + 1) * half, :] - b[s:s + 1, :], -jnp.inf)
                w = jnp.exp(rel) * q[hi * half:(hi + 1) * half, :] * k_in[s:s + 1, :]
                o_half[hi] = o_half[hi] + jnp.sum(w, axis=-1, keepdims=True) * v[s:s + 1, :]
                if hi == 0:
                    w = q_bot * k_top[s:s + 1, :]
                    o_half[1] = o_half[1] + jnp.sum(w, axis=-1, keepdims=True) * v[s:s + 1, :]
            o = jnp.concatenate(o_half, axis=0)

            k_dec = k_in * jnp.exp(b_last - b)
            u = lax.dot_general(v.astype(BF16), k_dec.astype(BF16), (((0,), (0,)), ((), ())),
                                preferred_element_type=F32)
            st_scr[h] = st * jnp.exp(b_last) + u

            out = _rms(o, gain) * _silu(og)
            o_ref[pl.ds(r0, c_rows), cs] = out.astype(o_ref.dtype)
        return carry

    lax.fori_loop(0, n_chunks, body, 0)


def _hgrn(proj, lb, gain, batch, seq):
    n = proj.shape[0]
    ts = min(512, seq)
    nsb = seq // ts
    wb = HG_WIDTH

    def col(cb):
        return pl.BlockSpec((ts, wb), lambda b, i: (b * nsb + i, cb))

    return pl.pallas_call(
        functools.partial(_hgrn_kernel, n_chunks=ts // HG_CHUNK),
        grid=(batch, nsb),
        in_specs=[col(COL_HQ // wb), col(COL_HF // wb), col(COL_HI // wb), col(COL_HG // wb),
                  pl.BlockSpec((1, wb), lambda b, i: (0, 0)),
                  pl.BlockSpec((1, HG_D), lambda b, i: (0, 0))],
        out_specs=pl.BlockSpec((ts, wb), lambda b, i: (b * nsb + i, 0)),
        out_shape=jax.ShapeDtypeStruct((n, wb), BF16),
        scratch_shapes=[pltpu.VMEM((HG_HEADS, HG_D, HG_D), F32)],
        compiler_params=_cparams(("arbitrary", "arbitrary")),
        name="hgrn2_scan",
    )(proj, proj, proj, proj, lb.reshape(1, wb), gain.reshape(1, HG_D))


def _split3(x):
    a = x.astype(BF16).astype(F32)
    b = (x - a).astype(BF16).astype(F32)
    c = (x - a - b).astype(BF16).astype(F32)
    return a, b, c


def _attn_kernel(slope_ref, q_ref, k_ref, v_ref, qg_ref, kg_ref, lam_ref, sg_ref, o_ref,
                 kn_scr, vt_scr, qa_scr, m0_scr, m1_scr, acc0_scr, acc1_scr,
                 *, tq, tk, seq, lam_init):
    i = pl.program_id(2)
    lane = lax.broadcasted_iota(jnp.int32, (1, LANES), 1)
    lo = lane < DA_D
    slope2 = slope_ref[...] * LOG2E

    def half_sums(sq):
        return (jnp.sum(jnp.where(lo, sq, 0.0), axis=-1, keepdims=True),
                jnp.sum(jnp.where(lo, 0.0, sq), axis=-1, keepdims=True))

    def norm_halves(x, g):
        s_lo, s_hi = half_sums(x * x)
        ms = jnp.where(lo, s_lo, s_hi) * (1.0 / DA_D)
        return x * lax.rsqrt(ms + EPS) * g

    @pl.when(i == 0)
    def _():
        rowk = lax.broadcasted_iota(jnp.int32, (tk, LANES), 0)
        lanek = lax.broadcasted_iota(jnp.int32, (tk, LANES), 1)
        ones_row = jnp.where(lax.broadcasted_iota(jnp.int32, (V_PAD, tk), 0) == 0, 1.0, 0.0).astype(BF16)

        def kb(j, carry):
            r = pl.multiple_of(j * tk, tk)
            kn = norm_halves(k_ref[pl.ds(r, tk), :].astype(F32), kg_ref[...])
            aug = jnp.where(lanek < 3, rowk >> 1, jnp.where(lanek < 6, rowk & 1, jnp.where(
                lanek < 9, j, jnp.where(lanek < AUG_COLS, 1, 0))))
            kn_scr[j, :, 0:LANES] = kn.astype(BF16)
            kn_scr[j, :, LANES:2 * LANES] = aug.astype(F32).astype(BF16)
            vt_scr[j, 0:DA_V, :] = v_ref[pl.ds(r, tk), :].astype(F32).T.astype(BF16)
            vt_scr[j, DA_V:DA_V + V_PAD, :] = ones_row
            return carry

        lax.fori_loop(0, seq // tk, kb, 0)

    qn = norm_halves(q_ref[...].astype(F32), qg_ref[...]) * (DA_D ** -0.5 * LOG2E)
    gq = jnp.max(jnp.abs(qg_ref[...]), axis=-1, keepdims=True)
    gk = jnp.max(jnp.abs(kg_ref[...]), axis=-1, keepdims=True)
    bound = (1.02 * DA_D * DA_D ** -0.5 * LOG2E) * gq * gk
    one_pass = bound[0, 0] * 2.0 <= BOUND_LIMIT
    use_bound = jnp.where(bound * 2.0 <= BOUND_LIMIT, 1.0, 0.0)

    sa, sb, sc = _split3(slope2)
    pieces = [2.0 * sa, 2.0 * sb, 2.0 * sc, sa, sb, sc, float(tk) * sa, float(tk) * sb, float(tk) * sc]
    bias_cols = jnp.zeros((1, LANES), F32)
    for col, piece in enumerate(pieces):
        bias_cols = jnp.where(lane == col, piece, bias_cols)
    lane_q = lax.broadcasted_iota(jnp.int32, (tq, LANES), 1)
    qpos = (lax.broadcasted_iota(jnp.int32, (tq, 1), 0) + i * tq).astype(F32)
    alibi_q = slope2[:, 0:1] * qpos
    ca, cb, cc = _split3(-(bound * use_bound + alibi_q))
    shift_cols = jnp.where(lane_q == 9, ca, jnp.where(lane_q == 10, cb, jnp.where(lane_q == 11, cc, bias_cols)))
    qa_scr[0, :, 0:LANES] = jnp.where(lo, qn, 0.0).astype(BF16)
    qa_scr[1, :, 0:LANES] = jnp.where(lo, 0.0, qn).astype(BF16)
    for c in range(2):
        qa_scr[c, :, LANES:2 * LANES] = shift_cols.astype(BF16)

    accs = (acc0_scr, acc1_scr)
    maxs = (m0_scr, m1_scr)
    for c in range(2):
        accs[c][...] = jnp.zeros_like(accs[c])
        maxs[c][...] = jnp.full_like(maxs[c], NEG_BIG)

    def scores(j, masked):
        kc = kn_scr[j]
        out = []
        for c in range(2):
            s = lax.dot_general(kc, qa_scr[c], (((1,), (1,)), ((), ())), preferred_element_type=F32)
            if masked:
                keyi = lax.broadcasted_iota(jnp.int32, (tk, tq), 0)
                qryi = lax.broadcasted_iota(jnp.int32, (tk, tq), 1)
                s = jnp.where(keyi <= qryi, s, NEG_BIG)
            out.append(s)
        return out

    def one_pass_update(blocks):
        sc_all = [scores(j, masked) for j, masked in blocks]
        for b, (j, _) in enumerate(blocks):
            vt = vt_scr[j]
            for c in range(2):
                p = jnp.exp2(sc_all[b][c]).astype(BF16)
                accs[c][...] += jnp.dot(vt, p, preferred_element_type=F32)

    def running_max_update(j, masked):
        vt = vt_scr[j]
        sc_j = scores(j, masked)
        for c in range(2):
            m_old = maxs[c][...]
            m_new = jnp.maximum(m_old, jnp.max(sc_j[c], axis=0, keepdims=True))
            p = jnp.exp2(sc_j[c] - m_new).astype(BF16)
            accs[c][...] = jnp.exp2(m_old - m_new) * accs[c][...] + jnp.dot(vt, p, preferred_element_type=F32)
            maxs[c][...] = m_new

    @pl.when(one_pass)
    def _():
        def pair(jj, carry):
            one_pass_update(((2 * jj, False), (2 * jj + 1, False)))
            return carry

        lax.fori_loop(0, i // 2, pair, 0)

        @pl.when(i % 2 == 1)
        def _():
            one_pass_update(((i - 1, False), (i, True)))

        @pl.when(i % 2 == 0)
        def _():
            one_pass_update(((i, True),))

    @pl.when(jnp.logical_not(one_pass))
    def _():
        def single(j, carry):
            running_max_update(j, False)
            return carry

        lax.fori_loop(0, i, single, 0)
        running_max_update(i, True)

    lv = lam_ref[...]
    lam = (jnp.exp(jnp.sum(lv[0:1, :] * lv[1:2, :], axis=-1, keepdims=True))
           - jnp.exp(jnp.sum(lv[2:3, :] * lv[3:4, :], axis=-1, keepdims=True)) + lam_init)
    o = (acc0_scr[0:DA_V, :] / acc0_scr[DA_V:DA_V + 1, :]
         - lam * (acc1_scr[0:DA_V, :] / acc1_scr[DA_V:DA_V + 1, :]))
    ms = jnp.mean(o * o, axis=0, keepdims=True)
    on = o * lax.rsqrt(ms + EPS) * sg_ref[...] * (1.0 - lam_init)
    o_ref[...] = on.T.astype(o_ref.dtype)


def _attn(proj, qg, kg, lam_vecs, sg, layer, batch, seq):
    n = proj.shape[0]
    tq = tk = min(512, seq)
    nqb = seq // tq
    lam_init = 0.8 - 0.6 * math.exp(-0.3 * layer)
    slopes = jnp.asarray(2.0 ** (-8.0 * jnp.arange(1, DA_HEADS + 1) / DA_HEADS), F32)
    slopes = jnp.broadcast_to(slopes[:, None, None], (DA_HEADS, 1, LANES))
    vec = lambda a: jnp.concatenate([a, a]).reshape(1, DA_V)
    sg_cols = jnp.broadcast_to(sg.astype(F32)[:, None], (DA_V, tq))
    v_rows = DA_V + V_PAD
    return pl.pallas_call(
        functools.partial(_attn_kernel, tq=tq, tk=tk, seq=seq, lam_init=lam_init),
        grid=(batch, DA_HEADS, nqb),
        in_specs=[
            pl.BlockSpec((None, 1, LANES), lambda b, h, i: (h, 0, 0)),
            pl.BlockSpec((tq, DA_V), lambda b, h, i: (b * nqb + i, COL_DQ // DA_V + h)),
            pl.BlockSpec((seq, DA_V), lambda b, h, i: (b, COL_DK // DA_V + h)),
            pl.BlockSpec((seq, DA_V), lambda b, h, i: (b, COL_DV // DA_V + h)),
            pl.BlockSpec((1, DA_V), lambda b, h, i: (0, 0)),
            pl.BlockSpec((1, DA_V), lambda b, h, i: (0, 0)),
            pl.BlockSpec((4, DA_D), lambda b, h, i: (0, 0)),
            pl.BlockSpec((DA_V, tq), lambda b, h, i: (0, 0)),
        ],
        out_specs=pl.BlockSpec((tq, DA_V), lambda b, h, i: (b * nqb + i, h)),
        out_shape=jax.ShapeDtypeStruct((n, DA_WIDTH), BF16),
        scratch_shapes=[
            pltpu.VMEM((seq // tk, tk, 2 * LANES), BF16),
            pltpu.VMEM((seq // tk, v_rows, tk), BF16),
            pltpu.VMEM((2, tq, 2 * LANES), BF16),
            pltpu.VMEM((1, tq), F32), pltpu.VMEM((1, tq), F32),
            pltpu.VMEM((v_rows, tq), F32), pltpu.VMEM((v_rows, tq), F32),
        ],
        compiler_params=_cparams(("arbitrary", "arbitrary", "arbitrary")),
        name="diff_attn",
    )(slopes, proj, proj, proj, vec(qg), vec(kg), lam_vecs, sg_cols)


def _merge_kernel(x_ref, ga0_ref, ga1_ref, gb0_ref, gb1_ref, ohg_ref, oda_ref, wpa_ref, wpb_ref, wo_ref,
                  mod_ref, gf_ref, xo_ref, h2_ref):
    a = jnp.dot(ohg_ref[...], wpa_ref[...], preferred_element_type=F32)
    b = jnp.dot(oda_ref[...], wpb_ref[...], preferred_element_type=F32)
    ga = jnp.concatenate([ga0_ref[...], ga1_ref[...]], axis=1).astype(F32)
    gb = jnp.concatenate([gb0_ref[...], gb1_ref[...]], axis=1).astype(F32)
    y = _sigmoid(ga) * a + _sigmoid(gb) * b
    mix = jnp.dot(y.astype(BF16), wo_ref[...], preferred_element_type=F32)
    xn = x_ref[...] + mod_ref[2:3, :] * mix
    xo_ref[...] = xn
    h = _rms(xn, gf_ref[...]) * (1.0 + mod_ref[4:5, :]) + mod_ref[3:4, :]
    h2_ref[...] = h.astype(h2_ref.dtype)


def _merge(x2, proj, o_hg, o_da, wpa, wpb, wo, mod, gain_ffn, seq, h2_dtype):
    n, d = x2.shape
    tm = min(512, seq)
    row = lambda w: pl.BlockSpec((tm, w), lambda i: (i, 0))
    full = lambda a: pl.BlockSpec(a.shape, lambda i: (0, 0))
    half = d // 2
    gate = lambda cb: pl.BlockSpec((tm, half), lambda i: (i, cb))
    return pl.pallas_call(
        _merge_kernel,
        grid=(n // tm,),
        in_specs=[
            row(d),
            gate(COL_GA // half), gate(COL_GA // half + 1),
            gate(COL_GB // half), gate(COL_GB // half + 1),
            row(HG_WIDTH), row(DA_WIDTH),
            full(wpa), full(wpb), full(wo),
            pl.BlockSpec((None, ADA_CHUNKS, d), lambda i: (i * tm // seq, 0, 0)),
            pl.BlockSpec((1, d), lambda i: (0, 0)),
        ],
        out_specs=[row(d), row(d)],
        out_shape=[jax.ShapeDtypeStruct((n, d), F32), jax.ShapeDtypeStruct((n, d), h2_dtype)],
        compiler_params=_cparams(("arbitrary",)),
        name="merge_outproj",
    )(x2, proj, proj, proj, proj, o_hg, o_da, wpa, wpb, wo, mod, gain_ffn.reshape(1, d))


def _ffn_kernel(h_ref, x_ref, mod_ref, w1_ref, w3_ref, w2_ref, o_ref, acc_scr):
    f = pl.program_id(1)

    @pl.when(f == 0)
    def _():
        acc_scr[...] = jnp.zeros_like(acc_scr)

    h = h_ref[...]
    g = _silu(jnp.dot(h, w1_ref[...], preferred_element_type=F32)) * jnp.dot(
        h, w3_ref[...], preferred_element_type=F32)
    acc_scr[...] += jnp.dot(g.astype(BF16), w2_ref[...], preferred_element_type=F32)

    @pl.when(f == pl.num_programs(1) - 1)
    def _():
        o_ref[...] = x_ref[...] + mod_ref[5:6, :] * acc_scr[...]


def _ffn(h2, x2, mod, w1, w3, w2, seq):
    n, d = x2.shape
    dff = w1.shape[1]
    tm = min(512, seq)
    tf = dff
    once = pl.Buffered(1)
    return pl.pallas_call(
        _ffn_kernel,
        grid=(n // tm, dff // tf),
        in_specs=[
            pl.BlockSpec((tm, d), lambda i, f: (i, 0)),
            pl.BlockSpec((tm, d), lambda i, f: (i, 0)),
            pl.BlockSpec((None, ADA_CHUNKS, d), lambda i, f: (i * tm // seq, 0, 0)),
            pl.BlockSpec((d, tf), lambda i, f: (0, f), pipeline_mode=once),
            pl.BlockSpec((d, tf), lambda i, f: (0, f), pipeline_mode=once),
            pl.BlockSpec((tf, d), lambda i, f: (f, 0), pipeline_mode=once),
        ],
        out_specs=pl.BlockSpec((tm, d), lambda i, f: (i, 0)),
        out_shape=jax.ShapeDtypeStruct((n, d), F32),
        scratch_shapes=[pltpu.VMEM((tm, d), F32)],
        compiler_params=_cparams(("arbitrary", "arbitrary")),
        name="dense_swiglu",
    )(h2, x2, mod, w1, w3, w2)


def _route_kernel(h_ref, r_ref, route_ref, cnt_ref, carry_scr):
    tm = h_ref.shape[0]

    @pl.when(pl.program_id(0) == 0)
    def _():
        carry_scr[...] = jnp.zeros_like(carry_scr)

    logits = jnp.dot(h_ref[...], r_ref[...], precision=lax.Precision.HIGHEST,
                     preferred_element_type=F32)
    lane = lax.broadcasted_iota(jnp.int32, (tm, LANES), 1).astype(F32)
    lg = jnp.where(lane < N_EXPERTS, logits, -jnp.inf)
    m0 = jnp.max(lg, axis=-1, keepdims=True)
    i0 = jnp.min(jnp.where(lg == m0, lane, float(LANES)), axis=-1, keepdims=True)
    lg1 = jnp.where(lane == i0, -jnp.inf, lg)
    m1 = jnp.max(lg1, axis=-1, keepdims=True)
    i1 = jnp.min(jnp.where(lg1 == m1, lane, float(LANES)), axis=-1, keepdims=True)
    e = jnp.exp(m1 - m0)
    w0 = 1.0 / (1.0 + e)
    w1 = e / (1.0 + e)

    sel = jnp.where(lane == i0, 1.0, jnp.where(lane == i1, 1.0, 0.0))
    rr = lax.broadcasted_iota(jnp.int32, (tm, tm), 0)
    cc = lax.broadcasted_iota(jnp.int32, (tm, tm), 1)
    tri = jnp.where(rr > cc, 1.0, 0.0).astype(BF16)
    before = jnp.dot(tri, sel.astype(BF16), preferred_element_type=F32) + carry_scr[...]
    r0 = jnp.sum(jnp.where(lane == i0, before, 0.0), axis=-1, keepdims=True)
    r1 = jnp.sum(jnp.where(lane == i1, before, 0.0), axis=-1, keepdims=True)
    carry_scr[...] += jnp.sum(sel, axis=0, keepdims=True)

    route = jnp.where(lane == 0.0, i0, jnp.where(lane == 1.0, i1, jnp.where(
        lane == 2.0, r0, jnp.where(lane == 3.0, r1, jnp.where(
            lane == 4.0, w0, jnp.where(lane == 5.0, w1, 0.0))))))
    route_ref[...] = route
    cnt_ref[...] = carry_scr[...]


def _route(h2, router):
    n, d = h2.shape
    tm = min(512, n)
    r_pad = jnp.zeros((d, LANES), F32).at[:, :N_EXPERTS].set(router)
    return pl.pallas_call(
        _route_kernel,
        grid=(n // tm,),
        in_specs=[pl.BlockSpec((tm, d), lambda i: (i, 0)),
                  pl.BlockSpec((d, LANES), lambda i: (0, 0))],
        out_specs=[pl.BlockSpec((tm, LANES), lambda i: (i, 0)),
                   pl.BlockSpec((1, LANES), lambda i: (0, 0))],
        out_shape=[jax.ShapeDtypeStruct((n, LANES), F32), jax.ShapeDtypeStruct((1, LANES), F32)],
        scratch_shapes=[pltpu.VMEM((1, LANES), F32)],
        compiler_params=_cparams(("arbitrary",)),
        name="moe_route",
    )(h2, r_pad)


def _row_copy(src_hbm, dst_ref, src_row, dst_row, sem):
    return pltpu.make_async_copy(src_hbm.at[pl.ds(src_row, 1)], dst_ref.at[pl.ds(dst_row, 1)], sem)


def _row_dma_sweep(n_tokens, start_token, wait_row):
    def issue(g, c):
        for u in range(ROW_DMA_UNROLL):
            start_token(g * ROW_DMA_UNROLL + u)
        return c

    def drain(g, c):
        for _ in range(2 * ROW_DMA_UNROLL):
            wait_row()
        return c

    lax.fori_loop(0, n_tokens // ROW_DMA_UNROLL, issue, 0)
    lax.fori_loop(0, n_tokens // ROW_DMA_UNROLL, drain, 0)


def _dispatch_kernel(pos_ref, h_ref, xs_init_hbm, xs_hbm, sem, *, tg):
    del xs_init_hbm

    def start_token(t):
        for k in range(2):
            _row_copy(h_ref, xs_hbm, t, pos_ref[2 * t + k], sem).start()

    _row_dma_sweep(tg, start_token, lambda: _row_copy(h_ref, xs_hbm, 0, 0, sem).wait())


def _dispatch(h2, pos, n_rows):
    n, d = h2.shape
    tg = min(512, n)
    xs0 = jnp.zeros((n_rows, d), h2.dtype)
    return pl.pallas_call(
        functools.partial(_dispatch_kernel, tg=tg),
        grid=(n // tg,),
        in_specs=[pl.BlockSpec((2 * tg,), lambda i: (i,), memory_space=pltpu.SMEM),
                  pl.BlockSpec((tg, d), lambda i: (i, 0)),
                  pl.BlockSpec(memory_space=pl.ANY)],
        out_specs=pl.BlockSpec(memory_space=pl.ANY),
        out_shape=jax.ShapeDtypeStruct((n_rows, d), h2.dtype),
        scratch_shapes=[pltpu.SemaphoreType.DMA(())],
        input_output_aliases={2: 0},
        compiler_params=_cparams(("arbitrary",)),
        name="moe_dispatch",
    )(pos, h2, xs0)


def _experts_kernel(te_ref, nu_ref, x_ref, w1_ref, w3_ref, w2_ref, y_ref, xb_scr, acc_scr):
    del te_ref
    i = pl.program_id(0)
    f = pl.program_id(1)
    used = i < nu_ref[0]

    @pl.when(jnp.logical_and(used, f == 0))
    def _():
        xb_scr[...] = x_ref[...].astype(BF16)
        acc_scr[...] = jnp.zeros_like(acc_scr)

    @pl.when(used)
    def _():
        xb = xb_scr[...]
        g = _silu(jnp.dot(xb, w1_ref[...], preferred_element_type=F32)) * jnp.dot(
            xb, w3_ref[...], preferred_element_type=F32)
        acc_scr[...] += jnp.dot(g.astype(BF16), w2_ref[...], preferred_element_type=F32)

    @pl.when(jnp.logical_and(used, f == pl.num_programs(1) - 1))
    def _():
        y_ref[...] = acc_scr[...]

    @pl.when(jnp.logical_and(jnp.logical_not(used), f == 0))
    def _():
        y_ref[...] = jnp.zeros_like(y_ref)


def _experts(xs, tile_expert, n_used, w1, w3, w2, tm):
    n_rows, d = xs.shape
    dff = w1.shape[2]
    tf = dff // 2
    nf = dff // tf
    n_tiles = n_rows // tm

    def wmap(which):
        def index_map(i, f, te, nu):
            ii = jnp.maximum(jnp.minimum(i, nu[0] - 1), 0)
            ff = jnp.where(i < nu[0], f, nf - 1)
            return (te[ii], 0, ff) if which == 0 else (te[ii], ff, 0)
        return index_map

    xmap = lambda i, f, te, nu: (jnp.maximum(jnp.minimum(i, nu[0] - 1), 0), 0)
    return pl.pallas_call(
        _experts_kernel,
        grid_spec=pltpu.PrefetchScalarGridSpec(
            num_scalar_prefetch=2,
            grid=(n_tiles, nf),
            in_specs=[
                pl.BlockSpec((tm, d), xmap),
                pl.BlockSpec((None, d, tf), wmap(0)),
                pl.BlockSpec((None, d, tf), wmap(0)),
                pl.BlockSpec((None, tf, d), wmap(1)),
            ],
            out_specs=pl.BlockSpec((tm, d), lambda i, f, te, nu: (i, 0)),
            scratch_shapes=[pltpu.VMEM((tm, d), BF16), pltpu.VMEM((tm, d), F32)],
        ),
        out_shape=jax.ShapeDtypeStruct((n_rows, d), F32),
        compiler_params=_cparams(("arbitrary", "arbitrary")),
        name="moe_experts",
    )(tile_expert, n_used, xs, w1, w3, w2)


def _combine_kernel(pos_ref, y_hbm, x_ref, route_ref, mod_ref, o_ref, ybuf, sem, *, tc):
    def start_token(t):
        for k in range(2):
            _row_copy(y_hbm, ybuf.at[k], pos_ref[2 * t + k], t, sem).start()

    _row_dma_sweep(tc, start_token, lambda: _row_copy(y_hbm, ybuf.at[0], 0, 0, sem).wait())
    route = route_ref[...]
    f = route[:, 4:5] * ybuf[0] + route[:, 5:6] * ybuf[1]
    o_ref[...] = x_ref[...] + mod_ref[5:6, :] * f


def _combine(y, pos, x2, route, mod, seq):
    n, d = x2.shape
    tc = min(256, seq)
    return pl.pallas_call(
        functools.partial(_combine_kernel, tc=tc),
        grid=(n // tc,),
        in_specs=[
            pl.BlockSpec((2 * tc,), lambda i: (i,), memory_space=pltpu.SMEM),
            pl.BlockSpec(memory_space=pl.ANY),
            pl.BlockSpec((tc, d), lambda i: (i, 0)),
            pl.BlockSpec((tc, LANES), lambda i: (i, 0)),
            pl.BlockSpec((None, ADA_CHUNKS, d), lambda i: (i * tc // seq, 0, 0)),
        ],
        out_specs=pl.BlockSpec((tc, d), lambda i: (i, 0)),
        out_shape=jax.ShapeDtypeStruct((n, d), F32),
        scratch_shapes=[pltpu.VMEM((2, tc, d), F32), pltpu.SemaphoreType.DMA(())],
        compiler_params=_cparams(("arbitrary",)),
        name="moe_combine",
    )(pos, y, x2, route, mod)


def _moe(h2, x2, mod, router, w1, w3, w2, seq):
    n, _ = x2.shape
    tm = min(512, n)
    route, counts = _route(h2, router)
    cnt = counts[0, :N_EXPERTS].astype(jnp.int32)
    padded = (cnt + tm - 1) // tm * tm
    ends = jnp.cumsum(padded)
    offs = ends - padded
    ids = route[:, 0:2].astype(jnp.int32)
    ranks = route[:, 2:4].astype(jnp.int32)
    pos = (offs[ids] + ranks).reshape(-1)
    n_tiles = 2 * n // tm + N_EXPERTS
    starts = jnp.arange(n_tiles, dtype=jnp.int32) * tm
    tile_expert = jnp.minimum(jnp.sum(starts[:, None] >= ends[None, :], axis=1), N_EXPERTS - 1).astype(jnp.int32)
    n_used = (ends[-1:] // tm).astype(jnp.int32)

    xs = _dispatch(h2, pos, n_tiles * tm)
    y = _experts(xs, tile_expert, n_used, w1, w3, w2, tm)
    return _combine(y, pos, x2, route, mod, seq)


def kernel(x, c, ada_w, ada_b, norm_mix_g, norm_ffn_g, w_in, hgrn_lb_logits, hgrn_norm_g, da_qnorm_g, da_knorm_g, da_lambda, da_subln_g, w_branch_a, w_branch_b, w_out, ffn_w1, ffn_w3, ffn_w2, moe_router, moe_w1, moe_w3, moe_w2):
    batch, seq, d = x.shape
    depth = ada_w.shape[0]
    n = batch * seq
    x2 = x.reshape(n, d)

    p = jax.nn.softmax(hgrn_lb_logits.astype(F32), axis=0)
    cum = jnp.cumsum(p, axis=0)
    lb_all = cum - cum[0:1]

    mods = _ada(c, ada_w, ada_b).reshape(depth, batch, ADA_CHUNKS, d)

    for l in range(depth):
        mod = mods[l]
        moe_layer = l % 2 == 1
        proj = _inproj(x2, mod, norm_mix_g[l], w_in[l].astype(BF16), seq)
        o_hg = _hgrn(proj, lb_all[l], hgrn_norm_g[l], batch, seq)
        o_da = _attn(proj, da_qnorm_g[l], da_knorm_g[l], da_lambda[l], da_subln_g[l], l, batch, seq)
        x2, h2 = _merge(x2, proj, o_hg, o_da, w_branch_a[l].astype(BF16), w_branch_b[l].astype(BF16),
                        w_out[l].astype(BF16), mod, norm_ffn_g[l], seq, F32 if moe_layer else BF16)
        if moe_layer:
            x2 = _moe(h2, x2, mod, moe_router[l // 2], moe_w1[l // 2].astype(BF16),
                      moe_w3[l // 2].astype(BF16), moe_w2[l // 2].astype(BF16), seq)
        else:
            x2 = _ffn(h2, x2, mod, ffn_w1[l // 2].astype(BF16), ffn_w3[l // 2].astype(BF16),
                      ffn_w2[l // 2].astype(BF16), seq)
    return x2.reshape(batch, seq, d)
```

```python
import functools
import math

import jax
import jax.numpy as jnp
from jax import lax
from jax.experimental import pallas as pl
from jax.experimental.pallas import tpu as pltpu

F32 = jnp.float32
BF16 = jnp.bfloat16
EPS = 1e-6

D_MODEL = 1024
HG_HEADS = 4
HG_D = 128
HG_WIDTH = HG_HEADS * HG_D
HG_CHUNK = 16
DA_HEADS = 4
DA_D = 64
DA_V = 2 * DA_D
DA_WIDTH = DA_HEADS * DA_V
N_EXPERTS = 8
ADA_CHUNKS = 6
LANES = 128
COL_HQ, COL_HF, COL_HI, COL_HG = 0, 512, 1024, 1536
COL_DQ, COL_DK, COL_DV = 2048, 2560, 3072
COL_GA, COL_GB = 3584, 4608
D_IN = 5632
NEG_BIG = -1e30
LOG2E = 1.4426950408889634
AUG_COLS = 12
V_PAD = 16
ROW_DMA_UNROLL = 8
BOUND_LIMIT = 80.0
VMEM_LIMIT = 56 * 1024 * 1024


def _cparams(sem):
    return pltpu.CompilerParams(dimension_semantics=sem, vmem_limit_bytes=VMEM_LIMIT)


def _sigmoid(x):
    return 1.0 / (1.0 + jnp.exp(-x))


def _silu(x):
    return x * _sigmoid(x)


def _rms(x, gain):
    ms = jnp.mean(x * x, axis=-1, keepdims=True)
    return x * lax.rsqrt(ms + EPS) * gain


def _ada_kernel(c_ref, w_ref, b_ref, o_ref):
    cs = _silu(c_ref[...])
    o_ref[...] = jnp.dot(cs, w_ref[...], precision=lax.Precision.HIGHEST,
                         preferred_element_type=F32) + b_ref[...]


def _ada(c, ada_w, ada_b):
    depth, d, n6 = ada_w.shape
    b = c.shape[0]
    tn = 1536
    return pl.pallas_call(
        _ada_kernel,
        grid=(depth, n6 // tn),
        in_specs=[
            pl.BlockSpec((b, d), lambda l, j: (0, 0)),
            pl.BlockSpec((None, d, tn), lambda l, j: (l, 0, j)),
            pl.BlockSpec((None, 1, tn), lambda l, j: (l, 0, j)),
        ],
        out_specs=pl.BlockSpec((None, b, tn), lambda l, j: (l, 0, j)),
        out_shape=jax.ShapeDtypeStruct((depth, b, n6), F32),
        compiler_params=_cparams(("arbitrary", "arbitrary")),
        name="ada_mod",
    )(c, ada_w, ada_b.reshape(depth, 1, n6))


def _inproj_kernel(x_ref, mod_ref, g_ref, w_ref, o_ref, h_scr):
    @pl.when(pl.program_id(1) == 0)
    def _():
        h = _rms(x_ref[...], g_ref[...]) * (1.0 + mod_ref[1:2, :]) + mod_ref[0:1, :]
        h_scr[...] = h.astype(BF16)

    o_ref[...] = jnp.dot(h_scr[...], w_ref[...], preferred_element_type=F32).astype(o_ref.dtype)


def _inproj(x2, mod, gain, w_in_bf, seq):
    n, d = x2.shape
    tm = min(512, seq)
    tn = D_IN // 2
    return pl.pallas_call(
        _inproj_kernel,
        grid=(n // tm, D_IN // tn),
        in_specs=[
            pl.BlockSpec((tm, d), lambda i, j: (i, 0)),
            pl.BlockSpec((None, ADA_CHUNKS, d), lambda i, j: (i * tm // seq, 0, 0)),
            pl.BlockSpec((1, d), lambda i, j: (0, 0)),
            pl.BlockSpec((d, tn), lambda i, j: (0, j)),
        ],
        out_specs=pl.BlockSpec((tm, tn), lambda i, j: (i, j)),
        out_shape=jax.ShapeDtypeStruct((n, D_IN), BF16),
        scratch_shapes=[pltpu.VMEM((tm, d), BF16)],
        compiler_params=_cparams(("arbitrary", "arbitrary")),
        name="norm_inproj",
    )(x2, mod, gain.reshape(1, d), w_in_bf)


def _hgrn_kernel(q_ref, z_ref, i_ref, og_ref, lb_ref, g_ref, o_ref, st_scr, *, n_chunks):
    c_rows = HG_CHUNK

    @pl.when(pl.program_id(1) == 0)
    def _():
        st_scr[...] = jnp.zeros_like(st_scr)

    half = c_rows // 2
    row = lax.broadcasted_iota(jnp.int32, (c_rows, HG_D), 0)
    row_half = lax.broadcasted_iota(jnp.int32, (half, HG_D), 0)
    gain = g_ref[...]
    lb = lb_ref[...]
    log_lb = jnp.log(lb)
    log1m_lb = jnp.log1p(-lb)
    one_m_lb = 1.0 - lb

    def body(c, carry):
        r0 = pl.multiple_of(c * c_rows, c_rows)
        for h in range(HG_HEADS):
            cs = slice(h * HG_D, (h + 1) * HG_D)
            z = z_ref[pl.ds(r0, c_rows), cs].astype(F32)
            q = q_ref[pl.ds(r0, c_rows), cs].astype(F32)
            v = i_ref[pl.ds(r0, c_rows), cs].astype(F32)
            og = og_ref[pl.ds(r0, c_rows), cs].astype(F32)

            e = jnp.exp(-jnp.abs(z))
            log_sig = jnp.minimum(z, 0.0) - jnp.log(1.0 + e)
            y = log1m_lb[:, cs] + log_sig
            a = log_lb[:, cs]
            log_f = jnp.maximum(a, y) + jnp.log(1.0 + jnp.exp(-jnp.abs(a - y)))
            sig_neg = jnp.where(z >= 0.0, e, 1.0) / (1.0 + e)
            k_in = one_m_lb[:, cs] * sig_neg

            b = log_f
            for d in (1, 2, 4, 8):
                b = b + jnp.where(row >= d, pltpu.roll(b, d, 0), 0.0)
            b_last = b[c_rows - 1:c_rows, :]

            st = st_scr[h]
            qe = (q * jnp.exp(b)).astype(BF16)
            o = lax.dot_general(qe, st.astype(BF16), (((1,), (1,)), ((), ())),
                                preferred_element_type=F32)
            o_half = [o[0:half, :], o[half:c_rows, :]]
            b_mid = b[half - 1:half, :]
            q_bot = q[half:c_rows, :] * jnp.exp(b[half:c_rows, :] - b_mid)
            k_top = k_in[0:half, :] * jnp.exp(b_mid - b[0:half, :])
            for s in range(c_rows):
                hi = s // half
                rel = jnp.where(row_half >= s - hi * half, b[hi * half:(hi + 1) * half, :] - b[s:s + 1, :], -jnp.inf)
                w = jnp.exp(rel) * q[hi * half:(hi + 1) * half, :] * k_in[s:s + 1, :]
                o_half[hi] = o_half[hi] + jnp.sum(w, axis=-1, keepdims=True) * v[s:s + 1, :]
                if hi == 0:
                    w = q_bot * k_top[s:s + 1, :]
                    o_half[1] = o_half[1] + jnp.sum(w, axis=-1, keepdims=True) * v[s:s + 1, :]
            o = jnp.concatenate(o_half, axis=0)

            k_dec = k_in * jnp.exp(b_last - b)
            u = lax.dot_general(v.astype(BF16), k_dec.astype(BF16), (((0,), (0,)), ((), ())),
                                preferred_element_type=F32)
            st_scr[h] = st * jnp.exp(b_last) + u

            out = _rms(o, gain) * _silu(og)
            o_ref[pl.ds(r0, c_rows), cs] = out.astype(o_ref.dtype)
        return carry

    lax.fori_loop(0, n_chunks, body, 0, unroll=2)


def _hgrn(proj, lb, gain, batch, seq):
    n = proj.shape[0]
    ts = min(512, seq)
    nsb = seq // ts
    wb = HG_WIDTH

    def col(cb):
        return pl.BlockSpec((ts, wb), lambda b, i: (b * nsb + i, cb))

    return pl.pallas_call(
        functools.partial(_hgrn_kernel, n_chunks=ts // HG_CHUNK),
        grid=(batch, nsb),
        in_specs=[col(COL_HQ // wb), col(COL_HF // wb), col(COL_HI // wb), col(COL_HG // wb),
                  pl.BlockSpec((1, wb), lambda b, i: (0, 0)),
                  pl.BlockSpec((1, HG_D), lambda b, i: (0, 0))],
        out_specs=pl.BlockSpec((ts, wb), lambda b, i: (b * nsb + i, 0)),
        out_shape=jax.ShapeDtypeStruct((n, wb), BF16),
        scratch_shapes=[pltpu.VMEM((HG_HEADS, HG_D, HG_D), F32)],
        compiler_params=_cparams(("arbitrary", "arbitrary")),
        name="hgrn2_scan",
    )(proj, proj, proj, proj, lb.reshape(1, wb), gain.reshape(1, HG_D))


def _split3(x):
    a = x.astype(BF16).astype(F32)
    b = (x - a).astype(BF16).astype(F32)
    c = (x - a - b).astype(BF16).astype(F32)
    return a, b, c


def _attn_kernel(slope_ref, q_ref, k_ref, v_ref, qg_ref, kg_ref, lam_ref, sg_ref, o_ref,
                 kn_scr, vt_scr, qa_scr, m0_scr, m1_scr, acc0_scr, acc1_scr,
                 *, tq, tk, seq, lam_init):
    i = pl.program_id(2)
    lane = lax.broadcasted_iota(jnp.int32, (1, LANES), 1)
    lo = lane < DA_D
    slope2 = slope_ref[...] * LOG2E

    def half_sums(sq):
        return (jnp.sum(jnp.where(lo, sq, 0.0), axis=-1, keepdims=True),
                jnp.sum(jnp.where(lo, 0.0, sq), axis=-1, keepdims=True))

    def norm_halves(x, g):
        s_lo, s_hi = half_sums(x * x)
        ms = jnp.where(lo, s_lo, s_hi) * (1.0 / DA_D)
        return x * lax.rsqrt(ms + EPS) * g

    @pl.when(i == 0)
    def _():
        rowk = lax.broadcasted_iota(jnp.int32, (tk, LANES), 0)
        lanek = lax.broadcasted_iota(jnp.int32, (tk, LANES), 1)
        ones_row = jnp.where(lax.broadcasted_iota(jnp.int32, (V_PAD, tk), 0) == 0, 1.0, 0.0).astype(BF16)

        def kb(j, carry):
            r = pl.multiple_of(j * tk, tk)
            kn = norm_halves(k_ref[pl.ds(r, tk), :].astype(F32), kg_ref[...])
            aug = jnp.where(lanek < 3, rowk >> 1, jnp.where(lanek < 6, rowk & 1, jnp.where(
                lanek < 9, j, jnp.where(lanek < AUG_COLS, 1, 0))))
            kn_scr[j, :, 0:LANES] = kn.astype(BF16)
            kn_scr[j, :, LANES:2 * LANES] = aug.astype(F32).astype(BF16)
            vt_scr[j, 0:DA_V, :] = v_ref[pl.ds(r, tk), :].astype(F32).T.astype(BF16)
            vt_scr[j, DA_V:DA_V + V_PAD, :] = ones_row
            return carry

        lax.fori_loop(0, seq // tk, kb, 0)

    qn = norm_halves(q_ref[...].astype(F32), qg_ref[...]) * (DA_D ** -0.5 * LOG2E)
    gq = jnp.max(jnp.abs(qg_ref[...]), axis=-1, keepdims=True)
    gk = jnp.max(jnp.abs(kg_ref[...]), axis=-1, keepdims=True)
    bound = (1.02 * DA_D * DA_D ** -0.5 * LOG2E) * gq * gk
    one_pass = bound[0, 0] * 2.0 <= BOUND_LIMIT
    use_bound = jnp.where(bound * 2.0 <= BOUND_LIMIT, 1.0, 0.0)

    sa, sb, sc = _split3(slope2)
    pieces = [2.0 * sa, 2.0 * sb, 2.0 * sc, sa, sb, sc, float(tk) * sa, float(tk) * sb, float(tk) * sc]
    bias_cols = jnp.zeros((1, LANES), F32)
    for col, piece in enumerate(pieces):
        bias_cols = jnp.where(lane == col, piece, bias_cols)
    lane_q = lax.broadcasted_iota(jnp.int32, (tq, LANES), 1)
    qpos = (lax.broadcasted_iota(jnp.int32, (tq, 1), 0) + i * tq).astype(F32)
    alibi_q = slope2[:, 0:1] * qpos
    ca, cb, cc = _split3(-(bound * use_bound + alibi_q))
    shift_cols = jnp.where(lane_q == 9, ca, jnp.where(lane_q == 10, cb, jnp.where(lane_q == 11, cc, bias_cols)))
    qa_scr[0, :, 0:LANES] = jnp.where(lo, qn, 0.0).astype(BF16)
    qa_scr[1, :, 0:LANES] = jnp.where(lo, 0.0, qn).astype(BF16)
    for c in range(2):
        qa_scr[c, :, LANES:2 * LANES] = shift_cols.astype(BF16)

    accs = (acc0_scr, acc1_scr)
    maxs = (m0_scr, m1_scr)
    for c in range(2):
        accs[c][...] = jnp.zeros_like(accs[c])
        maxs[c][...] = jnp.full_like(maxs[c], NEG_BIG)

    def scores(j, masked):
        kc = kn_scr[j]
        out = []
        for c in range(2):
            s = lax.dot_general(kc, qa_scr[c], (((1,), (1,)), ((), ())), preferred_element_type=F32)
            if masked:
                keyi = lax.broadcasted_iota(jnp.int32, (tk, tq), 0)
                qryi = lax.broadcasted_iota(jnp.int32, (tk, tq), 1)
                s = jnp.where(keyi <= qryi, s, NEG_BIG)
            out.append(s)
        return out

    def one_pass_update(blocks):
        sc_all = [scores(j, masked) for j, masked in blocks]
        for b, (j, _) in enumerate(blocks):
            vt = vt_scr[j]
            for c in range(2):
                p = jnp.exp2(sc_all[b][c]).astype(BF16)
                accs[c][...] += jnp.dot(vt, p, preferred_element_type=F32)

    def running_max_update(j, masked):
        vt = vt_scr[j]
        sc_j = scores(j, masked)
        for c in range(2):
            m_old = maxs[c][...]
            m_new = jnp.maximum(m_old, jnp.max(sc_j[c], axis=0, keepdims=True))
            p = jnp.exp2(sc_j[c] - m_new).astype(BF16)
            accs[c][...] = jnp.exp2(m_old - m_new) * accs[c][...] + jnp.dot(vt, p, preferred_element_type=F32)
            maxs[c][...] = m_new

    @pl.when(one_pass)
    def _():
        def pair(jj, carry):
            one_pass_update(((2 * jj, False), (2 * jj + 1, False)))
            return carry

        lax.fori_loop(0, i // 2, pair, 0)

        @pl.when(i % 2 == 1)
        def _():
            one_pass_update(((i - 1, False), (i, True)))

        @pl.when(i % 2 == 0)
        def _():
            one_pass_update(((i, True),))

    @pl.when(jnp.logical_not(one_pass))
    def _():
        def single(j, carry):
            running_max_update(j, False)
            return carry

        lax.fori_loop(0, i, single, 0)
        running_max_update(i, True)

    lv = lam_ref[...]
    lam = (jnp.exp(jnp.sum(lv[0:1, :] * lv[1:2, :], axis=-1, keepdims=True))
           - jnp.exp(jnp.sum(lv[2:3, :] * lv[3:4, :], axis=-1, keepdims=True)) + lam_init)
    o = (acc0_scr[0:DA_V, :] / acc0_scr[DA_V:DA_V + 1, :]
         - lam * (acc1_scr[0:DA_V, :] / acc1_scr[DA_V:DA_V + 1, :]))
    ms = jnp.mean(o * o, axis=0, keepdims=True)
    on = o * lax.rsqrt(ms + EPS) * sg_ref[...] * (1.0 - lam_init)
    o_ref[...] = on.T.astype(o_ref.dtype)


def _attn(proj, qg, kg, lam_vecs, sg, layer, batch, seq):
    n = proj.shape[0]
    tq = tk = min(512, seq)
    nqb = seq // tq
    lam_init = 0.8 - 0.6 * math.exp(-0.3 * layer)
    slopes = jnp.asarray(2.0 ** (-8.0 * jnp.arange(1, DA_HEADS + 1) / DA_HEADS), F32)
    slopes = jnp.broadcast_to(slopes[:, None, None], (DA_HEADS, 1, LANES))
    vec = lambda a: jnp.concatenate([a, a]).reshape(1, DA_V)
    sg_cols = jnp.broadcast_to(sg.astype(F32)[:, None], (DA_V, tq))
    v_rows = DA_V + V_PAD
    return pl.pallas_call(
        functools.partial(_attn_kernel, tq=tq, tk=tk, seq=seq, lam_init=lam_init),
        grid=(batch, DA_HEADS, nqb),
        in_specs=[
            pl.BlockSpec((None, 1, LANES), lambda b, h, i: (h, 0, 0)),
            pl.BlockSpec((tq, DA_V), lambda b, h, i: (b * nqb + i, COL_DQ // DA_V + h)),
            pl.BlockSpec((seq, DA_V), lambda b, h, i: (b, COL_DK // DA_V + h)),
            pl.BlockSpec((seq, DA_V), lambda b, h, i: (b, COL_DV // DA_V + h)),
            pl.BlockSpec((1, DA_V), lambda b, h, i: (0, 0)),
            pl.BlockSpec((1, DA_V), lambda b, h, i: (0, 0)),
            pl.BlockSpec((4, DA_D), lambda b, h, i: (0, 0)),
            pl.BlockSpec((DA_V, tq), lambda b, h, i: (0, 0)),
        ],
        out_specs=pl.BlockSpec((tq, DA_V), lambda b, h, i: (b * nqb + i, h)),
        out_shape=jax.ShapeDtypeStruct((n, DA_WIDTH), BF16),
        scratch_shapes=[
            pltpu.VMEM((seq // tk, tk, 2 * LANES), BF16),
            pltpu.VMEM((seq // tk, v_rows, tk), BF16),
            pltpu.VMEM((2, tq, 2 * LANES), BF16),
            pltpu.VMEM((1, tq), F32), pltpu.VMEM((1, tq), F32),
            pltpu.VMEM((v_rows, tq), F32), pltpu.VMEM((v_rows, tq), F32),
        ],
        compiler_params=_cparams(("arbitrary", "arbitrary", "arbitrary")),
        name="diff_attn",
    )(slopes, proj, proj, proj, vec(qg), vec(kg), lam_vecs, sg_cols)


def _merge_kernel(x_ref, ga0_ref, ga1_ref, gb0_ref, gb1_ref, ohg_ref, oda_ref, wpa_ref, wpb_ref, wo_ref,
                  mod_ref, gf_ref, xo_ref, h2_ref):
    a = jnp.dot(ohg_ref[...], wpa_ref[...], preferred_element_type=F32)
    b = jnp.dot(oda_ref[...], wpb_ref[...], preferred_element_type=F32)
    ga = jnp.concatenate([ga0_ref[...], ga1_ref[...]], axis=1).astype(F32)
    gb = jnp.concatenate([gb0_ref[...], gb1_ref[...]], axis=1).astype(F32)
    y = _sigmoid(ga) * a + _sigmoid(gb) * b
    mix = jnp.dot(y.astype(BF16), wo_ref[...], preferred_element_type=F32)
    xn = x_ref[...] + mod_ref[2:3, :] * mix
    xo_ref[...] = xn
    h = _rms(xn, gf_ref[...]) * (1.0 + mod_ref[4:5, :]) + mod_ref[3:4, :]
    h2_ref[...] = h.astype(h2_ref.dtype)


def _merge(x2, proj, o_hg, o_da, wpa, wpb, wo, mod, gain_ffn, seq, h2_dtype):
    n, d = x2.shape
    tm = min(512, seq)
    row = lambda w: pl.BlockSpec((tm, w), lambda i: (i, 0))
    full = lambda a: pl.BlockSpec(a.shape, lambda i: (0, 0))
    half = d // 2
    gate = lambda cb: pl.BlockSpec((tm, half), lambda i: (i, cb))
    return pl.pallas_call(
        _merge_kernel,
        grid=(n // tm,),
        in_specs=[
            row(d),
            gate(COL_GA // half), gate(COL_GA // half + 1),
            gate(COL_GB // half), gate(COL_GB // half + 1),
            row(HG_WIDTH), row(DA_WIDTH),
            full(wpa), full(wpb), full(wo),
            pl.BlockSpec((None, ADA_CHUNKS, d), lambda i: (i * tm // seq, 0, 0)),
            pl.BlockSpec((1, d), lambda i: (0, 0)),
        ],
        out_specs=[row(d), row(d)],
        out_shape=[jax.ShapeDtypeStruct((n, d), F32), jax.ShapeDtypeStruct((n, d), h2_dtype)],
        compiler_params=_cparams(("arbitrary",)),
        name="merge_outproj",
    )(x2, proj, proj, proj, proj, o_hg, o_da, wpa, wpb, wo, mod, gain_ffn.reshape(1, d))


def _ffn_kernel(h_ref, x_ref, mod_ref, w1_ref, w3_ref, w2_ref, o_ref, acc_scr):
    f = pl.program_id(1)

    @pl.when(f == 0)
    def _():
        acc_scr[...] = jnp.zeros_like(acc_scr)

    h = h_ref[...]
    g = _silu(jnp.dot(h, w1_ref[...], preferred_element_type=F32)) * jnp.dot(
        h, w3_ref[...], preferred_element_type=F32)
    acc_scr[...] += jnp.dot(g.astype(BF16), w2_ref[...], preferred_element_type=F32)

    @pl.when(f == pl.num_programs(1) - 1)
    def _():
        o_ref[...] = x_ref[...] + mod_ref[5:6, :] * acc_scr[...]


def _ffn(h2, x2, mod, w1, w3, w2, seq):
    n, d = x2.shape
    dff = w1.shape[1]
    tm = min(512, seq)
    tf = dff
    once = pl.Buffered(1)
    return pl.pallas_call(
        _ffn_kernel,
        grid=(n // tm, dff // tf),
        in_specs=[
            pl.BlockSpec((tm, d), lambda i, f: (i, 0)),
            pl.BlockSpec((tm, d), lambda i, f: (i, 0)),
            pl.BlockSpec((None, ADA_CHUNKS, d), lambda i, f: (i * tm // seq, 0, 0)),
            pl.BlockSpec((d, tf), lambda i, f: (0, f), pipeline_mode=once),
            pl.BlockSpec((d, tf), lambda i, f: (0, f), pipeline_mode=once),
            pl.BlockSpec((tf, d), lambda i, f: (f, 0), pipeline_mode=once),
        ],
        out_specs=pl.BlockSpec((tm, d), lambda i, f: (i, 0)),
        out_shape=jax.ShapeDtypeStruct((n, d), F32),
        scratch_shapes=[pltpu.VMEM((tm, d), F32)],
        compiler_params=_cparams(("arbitrary", "arbitrary")),
        name="dense_swiglu",
    )(h2, x2, mod, w1, w3, w2)


def _route_kernel(h_ref, r_ref, route_ref, cnt_ref, carry_scr):
    tm = h_ref.shape[0]

    @pl.when(pl.program_id(0) == 0)
    def _():
        carry_scr[...] = jnp.zeros_like(carry_scr)

    h = h_ref[...]
    h_hi = h.astype(BF16)
    h_lo = (h - h_hi.astype(F32)).astype(BF16)
    r = r_ref[...]
    r_hi = r.astype(BF16)
    r_lo = (r - r_hi.astype(F32)).astype(BF16)
    logits = (jnp.dot(h_hi, r_hi, preferred_element_type=F32) + jnp.dot(h_hi, r_lo, preferred_element_type=F32)
              + jnp.dot(h_lo, r_hi, preferred_element_type=F32))
    lane = lax.broadcasted_iota(jnp.int32, (tm, LANES), 1).astype(F32)
    lg = jnp.where(lane < N_EXPERTS, logits, -jnp.inf)
    m0 = jnp.max(lg, axis=-1, keepdims=True)
    i0 = jnp.min(jnp.where(lg == m0, lane, float(LANES)), axis=-1, keepdims=True)
    lg1 = jnp.where(lane == i0, -jnp.inf, lg)
    m1 = jnp.max(lg1, axis=-1, keepdims=True)
    i1 = jnp.min(jnp.where(lg1 == m1, lane, float(LANES)), axis=-1, keepdims=True)
    e = jnp.exp(m1 - m0)
    w0 = 1.0 / (1.0 + e)
    w1 = e / (1.0 + e)

    sel = jnp.where(lane == i0, 1.0, jnp.where(lane == i1, 1.0, 0.0))
    rr = lax.broadcasted_iota(jnp.int32, (tm, tm), 0)
    cc = lax.broadcasted_iota(jnp.int32, (tm, tm), 1)
    tri = jnp.where(rr > cc, 1.0, 0.0).astype(BF16)
    before = jnp.dot(tri, sel.astype(BF16), preferred_element_type=F32) + carry_scr[...]
    r0 = jnp.sum(jnp.where(lane == i0, before, 0.0), axis=-1, keepdims=True)
    r1 = jnp.sum(jnp.where(lane == i1, before, 0.0), axis=-1, keepdims=True)
    carry_scr[...] += jnp.sum(sel, axis=0, keepdims=True)

    route = jnp.where(lane == 0.0, i0, jnp.where(lane == 1.0, i1, jnp.where(
        lane == 2.0, r0, jnp.where(lane == 3.0, r1, jnp.where(
            lane == 4.0, w0, jnp.where(lane == 5.0, w1, 0.0))))))
    route_ref[...] = route
    cnt_ref[...] = carry_scr[...]


def _route(h2, router):
    n, d = h2.shape
    tm = min(512, n)
    r_pad = jnp.zeros((d, LANES), F32).at[:, :N_EXPERTS].set(router)
    return pl.pallas_call(
        _route_kernel,
        grid=(n // tm,),
        in_specs=[pl.BlockSpec((tm, d), lambda i: (i, 0)),
                  pl.BlockSpec((d, LANES), lambda i: (0, 0))],
        out_specs=[pl.BlockSpec((tm, LANES), lambda i: (i, 0)),
                   pl.BlockSpec((1, LANES), lambda i: (0, 0))],
        out_shape=[jax.ShapeDtypeStruct((n, LANES), F32), jax.ShapeDtypeStruct((1, LANES), F32)],
        scratch_shapes=[pltpu.VMEM((1, LANES), F32)],
        compiler_params=_cparams(("arbitrary",)),
        name="moe_route",
    )(h2, r_pad)


def _row_copy(src_hbm, dst_ref, src_row, dst_row, sem):
    return pltpu.make_async_copy(src_hbm.at[pl.ds(src_row, 1)], dst_ref.at[pl.ds(dst_row, 1)], sem)


def _row_dma_sweep(n_tokens, start_token, wait_row):
    def issue(g, c):
        for u in range(ROW_DMA_UNROLL):
            start_token(g * ROW_DMA_UNROLL + u)
        return c

    def drain(g, c):
        for _ in range(2 * ROW_DMA_UNROLL):
            wait_row()
        return c

    lax.fori_loop(0, n_tokens // ROW_DMA_UNROLL, issue, 0)
    lax.fori_loop(0, n_tokens // ROW_DMA_UNROLL, drain, 0)


def _dispatch_kernel(pos_ref, h_ref, xs_init_hbm, xs_hbm, sem, *, tg):
    del xs_init_hbm

    def start_token(t):
        for k in range(2):
            _row_copy(h_ref, xs_hbm, t, pos_ref[2 * t + k], sem).start()

    _row_dma_sweep(tg, start_token, lambda: _row_copy(h_ref, xs_hbm, 0, 0, sem).wait())


def _dispatch(h2, pos, n_rows):
    n, d = h2.shape
    tg = min(512, n)
    xs0 = jnp.zeros((n_rows, d), h2.dtype)
    return pl.pallas_call(
        functools.partial(_dispatch_kernel, tg=tg),
        grid=(n // tg,),
        in_specs=[pl.BlockSpec((2 * tg,), lambda i: (i,), memory_space=pltpu.SMEM),
                  pl.BlockSpec((tg, d), lambda i: (i, 0)),
                  pl.BlockSpec(memory_space=pl.ANY)],
        out_specs=pl.BlockSpec(memory_space=pl.ANY),
        out_shape=jax.ShapeDtypeStruct((n_rows, d), h2.dtype),
        scratch_shapes=[pltpu.SemaphoreType.DMA(())],
        input_output_aliases={2: 0},
        compiler_params=_cparams(("arbitrary",)),
        name="moe_dispatch",
    )(pos, h2, xs0)


def _experts_kernel(te_ref, nu_ref, x_ref, w1_ref, w3_ref, w2_ref, y_ref, xb_scr, acc_scr):
    del te_ref
    i = pl.program_id(0)
    f = pl.program_id(1)
    used = i < nu_ref[0]

    @pl.when(jnp.logical_and(used, f == 0))
    def _():
        xb_scr[...] = x_ref[...].astype(BF16)
        acc_scr[...] = jnp.zeros_like(acc_scr)

    @pl.when(used)
    def _():
        xb = xb_scr[...]
        g = _silu(jnp.dot(xb, w1_ref[...], preferred_element_type=F32)) * jnp.dot(
            xb, w3_ref[...], preferred_element_type=F32)
        acc_scr[...] += jnp.dot(g.astype(BF16), w2_ref[...], preferred_element_type=F32)

    @pl.when(jnp.logical_and(used, f == pl.num_programs(1) - 1))
    def _():
        y_ref[...] = acc_scr[...]

    @pl.when(jnp.logical_and(jnp.logical_not(used), f == 0))
    def _():
        y_ref[...] = jnp.zeros_like(y_ref)


def _experts(xs, tile_expert, n_used, w1, w3, w2, tm):
    n_rows, d = xs.shape
    dff = w1.shape[2]
    tf = dff // 2
    nf = dff // tf
    n_tiles = n_rows // tm

    def wmap(which):
        def index_map(i, f, te, nu):
            ii = jnp.maximum(jnp.minimum(i, nu[0] - 1), 0)
            ff = jnp.where(i < nu[0], f, nf - 1)
            return (te[ii], 0, ff) if which == 0 else (te[ii], ff, 0)
        return index_map

    xmap = lambda i, f, te, nu: (jnp.maximum(jnp.minimum(i, nu[0] - 1), 0), 0)
    return pl.pallas_call(
        _experts_kernel,
        grid_spec=pltpu.PrefetchScalarGridSpec(
            num_scalar_prefetch=2,
            grid=(n_tiles, nf),
            in_specs=[
                pl.BlockSpec((tm, d), xmap),
                pl.BlockSpec((None, d, tf), wmap(0)),
                pl.BlockSpec((None, d, tf), wmap(0)),
                pl.BlockSpec((None, tf, d), wmap(1)),
            ],
            out_specs=pl.BlockSpec((tm, d), lambda i, f, te, nu: (i, 0)),
            scratch_shapes=[pltpu.VMEM((tm, d), BF16), pltpu.VMEM((tm, d), F32)],
        ),
        out_shape=jax.ShapeDtypeStruct((n_rows, d), F32),
        compiler_params=_cparams(("arbitrary", "arbitrary")),
        name="moe_experts",
    )(tile_expert, n_used, xs, w1, w3, w2)


def _combine_kernel(pos_ref, y_hbm, x_ref, route_ref, mod_ref, o_ref, ybuf, sem, *, tc):
    def start_token(t):
        for k in range(2):
            _row_copy(y_hbm, ybuf.at[k], pos_ref[2 * t + k], t, sem).start()

    _row_dma_sweep(tc, start_token, lambda: _row_copy(y_hbm, ybuf.at[0], 0, 0, sem).wait())
    route = route_ref[...]
    f = route[:, 4:5] * ybuf[0] + route[:, 5:6] * ybuf[1]
    o_ref[...] = x_ref[...] + mod_ref[5:6, :] * f


def _combine(y, pos, x2, route, mod, seq):
    n, d = x2.shape
    tc = min(256, seq)
    return pl.pallas_call(
        functools.partial(_combine_kernel, tc=tc),
        grid=(n // tc,),
        in_specs=[
            pl.BlockSpec((2 * tc,), lambda i: (i,), memory_space=pltpu.SMEM),
            pl.BlockSpec(memory_space=pl.ANY),
            pl.BlockSpec((tc, d), lambda i: (i, 0)),
            pl.BlockSpec((tc, LANES), lambda i: (i, 0)),
            pl.BlockSpec((None, ADA_CHUNKS, d), lambda i: (i * tc // seq, 0, 0)),
        ],
        out_specs=pl.BlockSpec((tc, d), lambda i: (i, 0)),
        out_shape=jax.ShapeDtypeStruct((n, d), F32),
        scratch_shapes=[pltpu.VMEM((2, tc, d), F32), pltpu.SemaphoreType.DMA(())],
        compiler_params=_cparams(("arbitrary",)),
        name="moe_combine",
    )(pos, y, x2, route, mod)


def _moe(h2, x2, mod, router, w1, w3, w2, seq):
    n, _ = x2.shape
    tm = min(512, n)
    route, counts = _route(h2, router)
    cnt = counts[0, :N_EXPERTS].astype(jnp.int32)
    padded = (cnt + tm - 1) // tm * tm
    ends = jnp.cumsum(padded)
    offs = ends - padded
    ids = route[:, 0:2].astype(jnp.int32)
    ranks = route[:, 2:4].astype(jnp.int32)
    pos = (offs[ids] + ranks).reshape(-1)
    n_tiles = 2 * n // tm + N_EXPERTS
    starts = jnp.arange(n_tiles, dtype=jnp.int32) * tm
    tile_expert = jnp.minimum(jnp.sum(starts[:, None] >= ends[None, :], axis=1), N_EXPERTS - 1).astype(jnp.int32)
    n_used = (ends[-1:] // tm).astype(jnp.int32)

    xs = _dispatch(h2, pos, n_tiles * tm)
    y = _experts(xs, tile_expert, n_used, w1, w3, w2, tm)
    return _combine(y, pos, x2, route, mod, seq)


def kernel(x, c, ada_w, ada_b, norm_mix_g, norm_ffn_g, w_in, hgrn_lb_logits, hgrn_norm_g, da_qnorm_g, da_knorm_g, da_lambda, da_subln_g, w_branch_a, w_branch_b, w_out, ffn_w1, ffn_w3, ffn_w2, moe_router, moe_w1, moe_w3, moe_w2):
    batch, seq, d = x.shape
    depth = ada_w.shape[0]
    n = batch * seq
    x2 = x.reshape(n, d)

    p = jax.nn.softmax(hgrn_lb_logits.astype(F32), axis=0)
    cum = jnp.cumsum(p, axis=0)
    lb_all = cum - cum[0:1]

    mods = _ada(c, ada_w, ada_b).reshape(depth, batch, ADA_CHUNKS, d)

    for l in range(depth):
        mod = mods[l]
        moe_layer = l % 2 == 1
        proj = _inproj(x2, mod, norm_mix_g[l], w_in[l].astype(BF16), seq)
        o_hg = _hgrn(proj, lb_all[l], hgrn_norm_g[l], batch, seq)
        o_da = _attn(proj, da_qnorm_g[l], da_knorm_g[l], da_lambda[l], da_subln_g[l], l, batch, seq)
        x2, h2 = _merge(x2, proj, o_hg, o_da, w_branch_a[l].astype(BF16), w_branch_b[l].astype(BF16),
                        w_out[l].astype(BF16), mod, norm_ffn_g[l], seq, F32 if moe_layer else BF16)
        if moe_layer:
            x2 = _moe(h2, x2, mod, moe_router[l // 2], moe_w1[l // 2].astype(BF16),
                      moe_w3[l // 2].astype(BF16), moe_w2[l // 2].astype(BF16), seq)
        else:
            x2 = _ffn(h2, x2, mod, ffn_w1[l // 2].astype(BF16), ffn_w3[l // 2].astype(BF16),
                      ffn_w2[l // 2].astype(BF16), seq)
    return x2.reshape(batch, seq, d)
```

```python
import functools
import math

import jax
import jax.numpy as jnp
from jax import lax
from jax.experimental import pallas as pl
from jax.experimental.pallas import tpu as pltpu

F32 = jnp.float32
BF16 = jnp.bfloat16
EPS = 1e-6

D_MODEL = 1024
HG_HEADS = 4
HG_D = 128
HG_WIDTH = HG_HEADS * HG_D
HG_CHUNK = 16
DA_HEADS = 4
DA_D = 64
DA_V = 2 * DA_D
DA_WIDTH = DA_HEADS * DA_V
N_EXPERTS = 8
ADA_CHUNKS = 6
LANES = 128
COL_HQ, COL_HF, COL_HI, COL_HG = 0, 512, 1024, 1536
COL_DQ, COL_DK, COL_DV = 2048, 2560, 3072
COL_GA, COL_GB = 3584, 4608
D_IN = 5632
NEG_BIG = -1e30
LOG2E = 1.4426950408889634
V_PAD = 16
ROW_DMA_UNROLL = 8
BOUND_LIMIT = 80.0
VMEM_LIMIT = 56 * 1024 * 1024


def _cparams(sem):
    return pltpu.CompilerParams(dimension_semantics=sem, vmem_limit_bytes=VMEM_LIMIT)


def _sigmoid(x):
    return 1.0 / (1.0 + jnp.exp(-x))


def _silu(x):
    return x * _sigmoid(x)


def _rms(x, gain):
    ms = jnp.mean(x * x, axis=-1, keepdims=True)
    return x * lax.rsqrt(ms + EPS) * gain


def _ada_kernel(c_ref, w_ref, b_ref, o_ref):
    cs = _silu(c_ref[...])
    o_ref[...] = jnp.dot(cs, w_ref[...], precision=lax.Precision.HIGHEST,
                         preferred_element_type=F32) + b_ref[...]


def _ada(c, ada_w, ada_b):
    depth, d, n6 = ada_w.shape
    b = c.shape[0]
    tn = 1536
    return pl.pallas_call(
        _ada_kernel,
        grid=(depth, n6 // tn),
        in_specs=[
            pl.BlockSpec((b, d), lambda l, j: (0, 0)),
            pl.BlockSpec((None, d, tn), lambda l, j: (l, 0, j)),
            pl.BlockSpec((None, 1, tn), lambda l, j: (l, 0, j)),
        ],
        out_specs=pl.BlockSpec((None, b, tn), lambda l, j: (l, 0, j)),
        out_shape=jax.ShapeDtypeStruct((depth, b, n6), F32),
        compiler_params=_cparams(("arbitrary", "arbitrary")),
        name="ada_mod",
    )(c, ada_w, ada_b.reshape(depth, 1, n6))


def _inproj_kernel(x_ref, mod_ref, g_ref, w_ref, o_ref, *, tn):
    h = (_rms(x_ref[...], g_ref[...]) * (1.0 + mod_ref[1:2, :]) + mod_ref[0:1, :]).astype(BF16)
    for c in range(D_IN // tn):
        o_ref[:, c * tn:(c + 1) * tn] = jnp.dot(
            h, w_ref[:, c * tn:(c + 1) * tn], preferred_element_type=F32).astype(o_ref.dtype)


def _inproj(x2, mod, gain, w_in_bf, seq):
    n, d = x2.shape
    tm = min(512, seq)
    tn = D_IN // 2
    return pl.pallas_call(
        functools.partial(_inproj_kernel, tn=tn),
        grid=(n // tm,),
        in_specs=[
            pl.BlockSpec((tm, d), lambda i: (i, 0)),
            pl.BlockSpec((None, ADA_CHUNKS, d), lambda i: (i * tm // seq, 0, 0)),
            pl.BlockSpec((1, d), lambda i: (0, 0)),
            pl.BlockSpec((d, D_IN), lambda i: (0, 0), pipeline_mode=pl.Buffered(1)),
        ],
        out_specs=pl.BlockSpec((tm, D_IN), lambda i: (i, 0)),
        out_shape=jax.ShapeDtypeStruct((n, D_IN), BF16),
        compiler_params=_cparams(("arbitrary",)),
        name="norm_inproj",
    )(x2, mod, gain.reshape(1, d), w_in_bf)


def _hgrn_kernel(q_ref, z_ref, i_ref, og_ref, lb_ref, g_ref, o_ref, st_scr, *, n_chunks):
    c_rows = HG_CHUNK

    @pl.when(pl.program_id(1) == 0)
    def _():
        st_scr[...] = jnp.zeros_like(st_scr)

    half = c_rows // 2
    row = lax.broadcasted_iota(jnp.int32, (c_rows, HG_D), 0)
    row_half = lax.broadcasted_iota(jnp.int32, (half, HG_D), 0)
    gain = g_ref[...]
    lb = lb_ref[...]
    log_lb = jnp.log(lb)
    log1m_lb = jnp.log1p(-lb)
    one_m_lb = 1.0 - lb

    def body(c, carry):
        r0 = pl.multiple_of(c * c_rows, c_rows)
        for h in range(HG_HEADS):
            cs = slice(h * HG_D, (h + 1) * HG_D)
            z = z_ref[pl.ds(r0, c_rows), cs].astype(F32)
            q = q_ref[pl.ds(r0, c_rows), cs].astype(F32)
            v = i_ref[pl.ds(r0, c_rows), cs].astype(F32)
            og = og_ref[pl.ds(r0, c_rows), cs].astype(F32)

            e = jnp.exp(-jnp.abs(z))
            log_sig = jnp.minimum(z, 0.0) - jnp.log(1.0 + e)
            y = log1m_lb[:, cs] + log_sig
            a = log_lb[:, cs]
            log_f = jnp.maximum(a, y) + jnp.log(1.0 + jnp.exp(-jnp.abs(a - y)))
            sig_neg = jnp.where(z >= 0.0, e, 1.0) / (1.0 + e)
            k_in = one_m_lb[:, cs] * sig_neg

            b = log_f
            for d in (1, 2, 4, 8):
                b = b + jnp.where(row >= d, pltpu.roll(b, d, 0), 0.0)
            b_last = b[c_rows - 1:c_rows, :]

            st = st_scr[h]
            qe = (q * jnp.exp(b)).astype(BF16)
            o = lax.dot_general(qe, st.astype(BF16), (((1,), (1,)), ((), ())),
                                preferred_element_type=F32)
            o_half = [o[0:half, :], o[half:c_rows, :]]
            b_mid = b[half - 1:half, :]
            q_bot = q[half:c_rows, :] * jnp.exp(b[half:c_rows, :] - b_mid)
            k_top = k_in[0:half, :] * jnp.exp(b_mid - b[0:half, :])
            for s in range(c_rows):
                hi = s // half
                rel = jnp.where(row_half >= s - hi * half, b[hi * half:(hi + 1) * half, :] - b[s:s + 1, :], -jnp.inf)
                w = jnp.exp(rel) * q[hi * half:(hi + 1) * half, :] * k_in[s:s + 1, :]
                o_half[hi] = o_half[hi] + jnp.sum(w, axis=-1, keepdims=True) * v[s:s + 1, :]
                if hi == 0:
                    w = q_bot * k_top[s:s + 1, :]
                    o_half[1] = o_half[1] + jnp.sum(w, axis=-1, keepdims=True) * v[s:s + 1, :]
            o = jnp.concatenate(o_half, axis=0)

            k_dec = k_in * jnp.exp(b_last - b)
            u = lax.dot_general(v.astype(BF16), k_dec.astype(BF16), (((0,), (0,)), ((), ())),
                                preferred_element_type=F32)
            st_scr[h] = st * jnp.exp(b_last) + u

            out = _rms(o, gain) * _silu(og)
            o_ref[pl.ds(r0, c_rows), cs] = out.astype(o_ref.dtype)
        return carry

    lax.fori_loop(0, n_chunks, body, 0, unroll=2)


def _hgrn(proj, lb, gain, batch, seq):
    n = proj.shape[0]
    ts = min(512, seq)
    nsb = seq // ts
    wb = HG_WIDTH

    def col(cb):
        return pl.BlockSpec((ts, wb), lambda b, i: (b * nsb + i, cb))

    return pl.pallas_call(
        functools.partial(_hgrn_kernel, n_chunks=ts // HG_CHUNK),
        grid=(batch, nsb),
        in_specs=[col(COL_HQ // wb), col(COL_HF // wb), col(COL_HI // wb), col(COL_HG // wb),
                  pl.BlockSpec((1, wb), lambda b, i: (0, 0)),
                  pl.BlockSpec((1, HG_D), lambda b, i: (0, 0))],
        out_specs=pl.BlockSpec((ts, wb), lambda b, i: (b * nsb + i, 0)),
        out_shape=jax.ShapeDtypeStruct((n, wb), BF16),
        scratch_shapes=[pltpu.VMEM((HG_HEADS, HG_D, HG_D), F32)],
        compiler_params=_cparams(("arbitrary", "arbitrary")),
        name="hgrn2_scan",
    )(proj, proj, proj, proj, lb.reshape(1, wb), gain.reshape(1, HG_D))


def _split3(x):
    a = x.astype(BF16).astype(F32)
    b = (x - a).astype(BF16).astype(F32)
    c = (x - a - b).astype(BF16).astype(F32)
    return a, b, c


def _attn_kernel(slope_ref, q_ref, k_ref, v_ref, qg_ref, kg_ref, lam_ref, sg_ref, o_ref,
                 kn_scr, vt_scr, qaug_scr, qa_scr, m0_scr, m1_scr, acc0_scr, acc1_scr,
                 *, tq, tk, seq, lam_init):
    i = pl.program_id(2)
    lane = lax.broadcasted_iota(jnp.int32, (1, LANES), 1)
    lo = lane < DA_D
    slope2 = slope_ref[...] * LOG2E

    sa, sb, sc = _split3(slope2)
    half_ones = jnp.where((lax.broadcasted_iota(jnp.int32, (LANES, LANES), 0) < DA_D)
                          == (lax.broadcasted_iota(jnp.int32, (LANES, LANES), 1) < DA_D), 1.0, 0.0).astype(BF16)

    def norm_halves(x, g):
        sq = x * x
        sq_hi = sq.astype(BF16)
        sq_lo = (sq - sq_hi.astype(F32)).astype(BF16)
        ss = (jnp.dot(sq_hi, half_ones, preferred_element_type=F32)
              + jnp.dot(sq_lo, half_ones, preferred_element_type=F32))
        return x * lax.rsqrt(ss * (1.0 / DA_D) + EPS) * g

    def lane_table(shape, values):
        lane_idx = lax.broadcasted_iota(jnp.int32, shape, len(shape) - 1)
        out = jnp.zeros(shape, F32)
        for col, val in enumerate(values):
            out = jnp.where(lane_idx == col, val, out)
        return out

    @pl.when(i == 0)
    def _():
        rowk = lax.broadcasted_iota(jnp.int32, (tk, LANES), 0)
        rowq = lax.broadcasted_iota(jnp.int32, (tq, LANES), 0)
        ones_row = jnp.where(lax.broadcasted_iota(jnp.int32, (V_PAD, tk), 0) == 0, 1.0, 0.0).astype(BF16)
        key_hi = (rowk >> 1).astype(F32)
        key_lo = (rowk & 1).astype(F32)
        qry_hi = (rowq >> 1).astype(F32)
        qry_lo = (rowq & 1).astype(F32)
        qaug_scr[...] = lane_table((tq, LANES), [2.0 * sa, 2.0 * sb, 2.0 * sc, sa, sb, sc,
                                                 float(tk) * sa, float(tk) * sb, float(tk) * sc, 0.0, 0.0, 0.0,
                                                 qry_hi, qry_hi, qry_hi, qry_lo, qry_lo, qry_lo])

        key_cols = lane_table((tk, LANES), [key_hi, key_hi, key_hi, key_lo, key_lo, key_lo, 0.0, 0.0, 0.0, 1.0, 1.0, 1.0,
                                            -2.0 * sa, -2.0 * sb, -2.0 * sc, -sa, -sb, -sc])
        lanek = lax.broadcasted_iota(jnp.int32, (tk, LANES), 1)
        block_lanes = jnp.logical_and(lanek >= 6, lanek < 9)

        def kb(j, carry):
            r = pl.multiple_of(j * tk, tk)
            kn = norm_halves(k_ref[pl.ds(r, tk), :].astype(F32), kg_ref[...])
            aug = jnp.where(block_lanes, lax.convert_element_type(j, F32), key_cols)
            kn_scr[j, :, 0:LANES] = kn.astype(BF16)
            kn_scr[j, :, LANES:2 * LANES] = aug.astype(BF16)
            vt_scr[j, 0:DA_V, :] = v_ref[pl.ds(r, tk), :].astype(F32).T.astype(BF16)
            vt_scr[j, DA_V:DA_V + V_PAD, :] = ones_row
            return carry

        lax.fori_loop(0, seq // tk, kb, 0)

    qn = norm_halves(q_ref[...].astype(F32), qg_ref[...]) * (DA_D ** -0.5 * LOG2E)
    gq = jnp.max(jnp.abs(qg_ref[...]), axis=-1, keepdims=True)
    gk = jnp.max(jnp.abs(kg_ref[...]), axis=-1, keepdims=True)
    bound = (1.02 * DA_D * DA_D ** -0.5 * LOG2E) * gq * gk
    one_pass = bound[0, 0] * 2.0 <= BOUND_LIMIT
    use_bound = jnp.where(bound * 2.0 <= BOUND_LIMIT, 1.0, 0.0)

    ca, cb, cc = _split3(-(bound * use_bound + slope2 * (i * tq).astype(F32)))
    block_cols = lane_table((1, LANES), [0.0] * 9 + [ca, cb, cc])
    lane_q = lax.broadcasted_iota(jnp.int32, (tq, LANES), 1)
    shift_cols = jnp.where(jnp.logical_and(lane_q >= 9, lane_q < 12), block_cols, qaug_scr[...]).astype(BF16)
    qa_scr[0, :, 0:LANES] = jnp.where(lo, qn, 0.0).astype(BF16)
    qa_scr[1, :, 0:LANES] = jnp.where(lo, 0.0, qn).astype(BF16)
    for c in range(2):
        qa_scr[c, :, LANES:2 * LANES] = shift_cols

    accs = (acc0_scr, acc1_scr)
    maxs = (m0_scr, m1_scr)
    for c in range(2):
        accs[c][...] = jnp.zeros_like(accs[c])
        maxs[c][...] = jnp.full_like(maxs[c], NEG_BIG)

    def scores(j, masked):
        kc = kn_scr[j]
        out = []
        for c in range(2):
            s = lax.dot_general(kc, qa_scr[c], (((1,), (1,)), ((), ())), preferred_element_type=F32)
            if masked:
                keyi = lax.broadcasted_iota(jnp.int32, (tk, tq), 0)
                qryi = lax.broadcasted_iota(jnp.int32, (tk, tq), 1)
                s = jnp.where(keyi <= qryi, s, NEG_BIG)
            out.append(s)
        return out

    def one_pass_update(blocks):
        sc_all = [scores(j, masked) for j, masked in blocks]
        for b, (j, _) in enumerate(blocks):
            vt = vt_scr[j]
            for c in range(2):
                p = jnp.exp2(sc_all[b][c]).astype(BF16)
                accs[c][...] += jnp.dot(vt, p, preferred_element_type=F32)

    def running_max_update(j, masked):
        vt = vt_scr[j]
        sc_j = scores(j, masked)
        for c in range(2):
            m_old = maxs[c][...]
            m_new = jnp.maximum(m_old, jnp.max(sc_j[c], axis=0, keepdims=True))
            p = jnp.exp2(sc_j[c] - m_new).astype(BF16)
            accs[c][...] = jnp.exp2(m_old - m_new) * accs[c][...] + jnp.dot(vt, p, preferred_element_type=F32)
            maxs[c][...] = m_new

    @pl.when(one_pass)
    def _():
        def pair(jj, carry):
            one_pass_update(((2 * jj, False), (2 * jj + 1, False)))
            return carry

        lax.fori_loop(0, i // 2, pair, 0)

        @pl.when(i % 2 == 1)
        def _():
            one_pass_update(((i - 1, False), (i, True)))

        @pl.when(i % 2 == 0)
        def _():
            one_pass_update(((i, True),))

    @pl.when(jnp.logical_not(one_pass))
    def _():
        def single(j, carry):
            running_max_update(j, False)
            return carry

        lax.fori_loop(0, i, single, 0)
        running_max_update(i, True)

    lv = lam_ref[...]
    lam = (jnp.exp(jnp.sum(lv[0:1, :] * lv[1:2, :], axis=-1, keepdims=True))
           - jnp.exp(jnp.sum(lv[2:3, :] * lv[3:4, :], axis=-1, keepdims=True)) + lam_init)
    o = (acc0_scr[0:DA_V, :] / acc0_scr[DA_V:DA_V + 1, :]
         - lam * (acc1_scr[0:DA_V, :] / acc1_scr[DA_V:DA_V + 1, :]))
    ms = jnp.mean(o * o, axis=0, keepdims=True)
    on = o * lax.rsqrt(ms + EPS) * sg_ref[...] * (1.0 - lam_init)
    o_ref[...] = on.T.astype(o_ref.dtype)


def _attn(proj, qg, kg, lam_vecs, sg, layer, batch, seq):
    n = proj.shape[0]
    tq = tk = min(512, seq)
    nqb = seq // tq
    lam_init = 0.8 - 0.6 * math.exp(-0.3 * layer)
    slopes = jnp.asarray(2.0 ** (-8.0 * jnp.arange(1, DA_HEADS + 1) / DA_HEADS), F32)
    slopes = jnp.broadcast_to(slopes[:, None, None], (DA_HEADS, 1, LANES))
    vec = lambda a: jnp.concatenate([a, a]).reshape(1, DA_V)
    sg_cols = jnp.broadcast_to(sg.astype(F32)[:, None], (DA_V, tq))
    v_rows = DA_V + V_PAD
    return pl.pallas_call(
        functools.partial(_attn_kernel, tq=tq, tk=tk, seq=seq, lam_init=lam_init),
        grid=(batch, DA_HEADS, nqb),
        in_specs=[
            pl.BlockSpec((None, 1, LANES), lambda b, h, i: (h, 0, 0)),
            pl.BlockSpec((tq, DA_V), lambda b, h, i: (b * nqb + i, COL_DQ // DA_V + h)),
            pl.BlockSpec((seq, DA_V), lambda b, h, i: (b, COL_DK // DA_V + h)),
            pl.BlockSpec((seq, DA_V), lambda b, h, i: (b, COL_DV // DA_V + h)),
            pl.BlockSpec((1, DA_V), lambda b, h, i: (0, 0)),
            pl.BlockSpec((1, DA_V), lambda b, h, i: (0, 0)),
            pl.BlockSpec((4, DA_D), lambda b, h, i: (0, 0)),
            pl.BlockSpec((DA_V, tq), lambda b, h, i: (0, 0)),
        ],
        out_specs=pl.BlockSpec((tq, DA_V), lambda b, h, i: (b * nqb + i, h)),
        out_shape=jax.ShapeDtypeStruct((n, DA_WIDTH), BF16),
        scratch_shapes=[
            pltpu.VMEM((seq // tk, tk, 2 * LANES), BF16),
            pltpu.VMEM((seq // tk, v_rows, tk), BF16),
            pltpu.VMEM((tq, LANES), F32),
            pltpu.VMEM((2, tq, 2 * LANES), BF16),
            pltpu.VMEM((1, tq), F32), pltpu.VMEM((1, tq), F32),
            pltpu.VMEM((v_rows, tq), F32), pltpu.VMEM((v_rows, tq), F32),
        ],
        compiler_params=_cparams(("arbitrary", "arbitrary", "arbitrary")),
        name="diff_attn",
    )(slopes, proj, proj, proj, vec(qg), vec(kg), lam_vecs, sg_cols)


def _merge_kernel(x_ref, ga0_ref, ga1_ref, gb0_ref, gb1_ref, ohg_ref, oda_ref, wpa_ref, wpb_ref, wo_ref,
                  mod_ref, gf_ref, xo_ref, h2_ref):
    a = jnp.dot(ohg_ref[...], wpa_ref[...], preferred_element_type=F32)
    b = jnp.dot(oda_ref[...], wpb_ref[...], preferred_element_type=F32)
    ga = jnp.concatenate([ga0_ref[...], ga1_ref[...]], axis=1).astype(F32)
    gb = jnp.concatenate([gb0_ref[...], gb1_ref[...]], axis=1).astype(F32)
    y = _sigmoid(ga) * a + _sigmoid(gb) * b
    mix = jnp.dot(y.astype(BF16), wo_ref[...], preferred_element_type=F32)
    xn = x_ref[...] + mod_ref[2:3, :] * mix
    xo_ref[...] = xn
    h = _rms(xn, gf_ref[...]) * (1.0 + mod_ref[4:5, :]) + mod_ref[3:4, :]
    h2_ref[...] = h.astype(h2_ref.dtype)


def _merge(x2, proj, o_hg, o_da, wpa, wpb, wo, mod, gain_ffn, seq, h2_dtype):
    n, d = x2.shape
    tm = min(512, seq)
    row = lambda w: pl.BlockSpec((tm, w), lambda i: (i, 0))
    full = lambda a: pl.BlockSpec(a.shape, lambda i: (0, 0))
    half = d // 2
    gate = lambda cb: pl.BlockSpec((tm, half), lambda i: (i, cb))
    return pl.pallas_call(
        _merge_kernel,
        grid=(n // tm,),
        in_specs=[
            row(d),
            gate(COL_GA // half), gate(COL_GA // half + 1),
            gate(COL_GB // half), gate(COL_GB // half + 1),
            row(HG_WIDTH), row(DA_WIDTH),
            full(wpa), full(wpb), full(wo),
            pl.BlockSpec((None, ADA_CHUNKS, d), lambda i: (i * tm // seq, 0, 0)),
            pl.BlockSpec((1, d), lambda i: (0, 0)),
        ],
        out_specs=[row(d), row(d)],
        out_shape=[jax.ShapeDtypeStruct((n, d), F32), jax.ShapeDtypeStruct((n, d), h2_dtype)],
        compiler_params=_cparams(("arbitrary",)),
        name="merge_outproj",
    )(x2, proj, proj, proj, proj, o_hg, o_da, wpa, wpb, wo, mod, gain_ffn.reshape(1, d))


def _ffn_kernel(h_ref, x_ref, mod_ref, w1_ref, w3_ref, w2_ref, o_ref, acc_scr):
    f = pl.program_id(1)

    @pl.when(f == 0)
    def _():
        acc_scr[...] = jnp.zeros_like(acc_scr)

    h = h_ref[...]
    g = _silu(jnp.dot(h, w1_ref[...], preferred_element_type=F32)) * jnp.dot(
        h, w3_ref[...], preferred_element_type=F32)
    acc_scr[...] += jnp.dot(g.astype(BF16), w2_ref[...], preferred_element_type=F32)

    @pl.when(f == pl.num_programs(1) - 1)
    def _():
        o_ref[...] = x_ref[...] + mod_ref[5:6, :] * acc_scr[...]


def _ffn(h2, x2, mod, w1, w3, w2, seq):
    n, d = x2.shape
    dff = w1.shape[1]
    tm = min(512, seq)
    tf = dff
    once = pl.Buffered(1)
    return pl.pallas_call(
        _ffn_kernel,
        grid=(n // tm, dff // tf),
        in_specs=[
            pl.BlockSpec((tm, d), lambda i, f: (i, 0)),
            pl.BlockSpec((tm, d), lambda i, f: (i, 0)),
            pl.BlockSpec((None, ADA_CHUNKS, d), lambda i, f: (i * tm // seq, 0, 0)),
            pl.BlockSpec((d, tf), lambda i, f: (0, f), pipeline_mode=once),
            pl.BlockSpec((d, tf), lambda i, f: (0, f), pipeline_mode=once),
            pl.BlockSpec((tf, d), lambda i, f: (f, 0), pipeline_mode=once),
        ],
        out_specs=pl.BlockSpec((tm, d), lambda i, f: (i, 0)),
        out_shape=jax.ShapeDtypeStruct((n, d), F32),
        scratch_shapes=[pltpu.VMEM((tm, d), F32)],
        compiler_params=_cparams(("arbitrary", "arbitrary")),
        name="dense_swiglu",
    )(h2, x2, mod, w1, w3, w2)


def _route_kernel(h_ref, r_ref, route_ref, cnt_ref, carry_scr):
    tm = h_ref.shape[0]

    @pl.when(pl.program_id(0) == 0)
    def _():
        carry_scr[...] = jnp.zeros_like(carry_scr)

    h = h_ref[...]
    h_hi = h.astype(BF16)
    h_lo = (h - h_hi.astype(F32)).astype(BF16)
    r = r_ref[...]
    r_hi = r.astype(BF16)
    r_lo = (r - r_hi.astype(F32)).astype(BF16)
    logits = (jnp.dot(h_hi, r_hi, preferred_element_type=F32) + jnp.dot(h_hi, r_lo, preferred_element_type=F32)
              + jnp.dot(h_lo, r_hi, preferred_element_type=F32))
    lane = lax.broadcasted_iota(jnp.int32, (tm, LANES), 1).astype(F32)
    lg = jnp.where(lane < N_EXPERTS, logits, -jnp.inf)
    m0 = jnp.max(lg, axis=-1, keepdims=True)
    i0 = jnp.min(jnp.where(lg == m0, lane, float(LANES)), axis=-1, keepdims=True)
    lg1 = jnp.where(lane == i0, -jnp.inf, lg)
    m1 = jnp.max(lg1, axis=-1, keepdims=True)
    i1 = jnp.min(jnp.where(lg1 == m1, lane, float(LANES)), axis=-1, keepdims=True)
    e = jnp.exp(m1 - m0)
    w0 = 1.0 / (1.0 + e)
    w1 = e / (1.0 + e)

    sel = jnp.where(lane == i0, 1.0, jnp.where(lane == i1, 1.0, 0.0))
    rr = lax.broadcasted_iota(jnp.int32, (tm, tm), 0)
    cc = lax.broadcasted_iota(jnp.int32, (tm, tm), 1)
    tri = jnp.where(rr > cc, 1.0, 0.0).astype(BF16)
    before = jnp.dot(tri, sel.astype(BF16), preferred_element_type=F32) + carry_scr[...]
    r0 = jnp.sum(jnp.where(lane == i0, before, 0.0), axis=-1, keepdims=True)
    r1 = jnp.sum(jnp.where(lane == i1, before, 0.0), axis=-1, keepdims=True)
    carry_scr[...] += jnp.sum(sel, axis=0, keepdims=True)

    route = jnp.where(lane == 0.0, i0, jnp.where(lane == 1.0, i1, jnp.where(
        lane == 2.0, r0, jnp.where(lane == 3.0, r1, jnp.where(
            lane == 4.0, w0, jnp.where(lane == 5.0, w1, 0.0))))))
    route_ref[...] = route
    cnt_ref[...] = carry_scr[...]


def _route(h2, router):
    n, d = h2.shape
    tm = min(512, n)
    r_pad = jnp.zeros((d, LANES), F32).at[:, :N_EXPERTS].set(router)
    return pl.pallas_call(
        _route_kernel,
        grid=(n // tm,),
        in_specs=[pl.BlockSpec((tm, d), lambda i: (i, 0)),
                  pl.BlockSpec((d, LANES), lambda i: (0, 0))],
        out_specs=[pl.BlockSpec((tm, LANES), lambda i: (i, 0)),
                   pl.BlockSpec((1, LANES), lambda i: (0, 0))],
        out_shape=[jax.ShapeDtypeStruct((n, LANES), F32), jax.ShapeDtypeStruct((1, LANES), F32)],
        scratch_shapes=[pltpu.VMEM((1, LANES), F32)],
        compiler_params=_cparams(("arbitrary",)),
        name="moe_route",
    )(h2, r_pad)


def _row_copy(src_hbm, dst_ref, src_row, dst_row, sem):
    return pltpu.make_async_copy(src_hbm.at[pl.ds(src_row, 1)], dst_ref.at[pl.ds(dst_row, 1)], sem)


def _row_dma_issue(n_tokens, start_token):
    def issue(g, c):
        for u in range(ROW_DMA_UNROLL):
            start_token(g * ROW_DMA_UNROLL + u)
        return c

    lax.fori_loop(0, n_tokens // ROW_DMA_UNROLL, issue, 0)


def _row_dma_drain(n_tokens, wait_row):
    def drain(g, c):
        for _ in range(2 * ROW_DMA_UNROLL):
            wait_row()
        return c

    lax.fori_loop(0, n_tokens // ROW_DMA_UNROLL, drain, 0)


def _dispatch_kernel(pos_ref, h_ref, xs_init_hbm, xs_hbm, sem, *, tg):
    del xs_init_hbm

    def start_token(t):
        for k in range(2):
            _row_copy(h_ref, xs_hbm, t, pos_ref[2 * t + k], sem).start()

    _row_dma_issue(tg, start_token)
    _row_dma_drain(tg, lambda: _row_copy(h_ref, xs_hbm, 0, 0, sem).wait())


def _dispatch(h2, pos, n_rows):
    n, d = h2.shape
    tg = min(512, n)
    xs0 = jnp.zeros((n_rows, d), h2.dtype)
    return pl.pallas_call(
        functools.partial(_dispatch_kernel, tg=tg),
        grid=(n // tg,),
        in_specs=[pl.BlockSpec((2 * tg,), lambda i: (i,), memory_space=pltpu.SMEM),
                  pl.BlockSpec((tg, d), lambda i: (i, 0)),
                  pl.BlockSpec(memory_space=pl.ANY)],
        out_specs=pl.BlockSpec(memory_space=pl.ANY),
        out_shape=jax.ShapeDtypeStruct((n_rows, d), h2.dtype),
        scratch_shapes=[pltpu.SemaphoreType.DMA(())],
        input_output_aliases={2: 0},
        compiler_params=_cparams(("arbitrary",)),
        name="moe_dispatch",
    )(pos, h2, xs0)


def _experts_kernel(te_ref, nu_ref, x_ref, w1_ref, w3_ref, w2_ref, y_ref, xb_scr, acc_scr):
    del te_ref
    i = pl.program_id(0)
    f = pl.program_id(1)
    used = i < nu_ref[0]

    @pl.when(jnp.logical_and(used, f == 0))
    def _():
        xb_scr[...] = x_ref[...].astype(BF16)
        acc_scr[...] = jnp.zeros_like(acc_scr)

    @pl.when(used)
    def _():
        xb = xb_scr[...]
        g = _silu(jnp.dot(xb, w1_ref[...], preferred_element_type=F32)) * jnp.dot(
            xb, w3_ref[...], preferred_element_type=F32)
        acc_scr[...] += jnp.dot(g.astype(BF16), w2_ref[...], preferred_element_type=F32)

    @pl.when(jnp.logical_and(used, f == pl.num_programs(1) - 1))
    def _():
        y_ref[...] = acc_scr[...]

    @pl.when(jnp.logical_and(jnp.logical_not(used), f == 0))
    def _():
        y_ref[...] = jnp.zeros_like(y_ref)


def _experts(xs, tile_expert, n_used, w1, w3, w2, tm):
    n_rows, d = xs.shape
    dff = w1.shape[2]
    tf = dff // 2
    nf = dff // tf
    n_tiles = n_rows // tm

    def wmap(which):
        def index_map(i, f, te, nu):
            ii = jnp.maximum(jnp.minimum(i, nu[0] - 1), 0)
            ff = jnp.where(i < nu[0], f, nf - 1)
            return (te[ii], 0, ff) if which == 0 else (te[ii], ff, 0)
        return index_map

    xmap = lambda i, f, te, nu: (jnp.maximum(jnp.minimum(i, nu[0] - 1), 0), 0)
    return pl.pallas_call(
        _experts_kernel,
        grid_spec=pltpu.PrefetchScalarGridSpec(
            num_scalar_prefetch=2,
            grid=(n_tiles, nf),
            in_specs=[
                pl.BlockSpec((tm, d), xmap),
                pl.BlockSpec((None, d, tf), wmap(0)),
                pl.BlockSpec((None, d, tf), wmap(0)),
                pl.BlockSpec((None, tf, d), wmap(1)),
            ],
            out_specs=pl.BlockSpec((tm, d), lambda i, f, te, nu: (i, 0)),
            scratch_shapes=[pltpu.VMEM((tm, d), BF16), pltpu.VMEM((tm, d), F32)],
        ),
        out_shape=jax.ShapeDtypeStruct((n_rows, d), F32),
        compiler_params=_cparams(("arbitrary", "arbitrary")),
        name="moe_experts",
    )(tile_expert, n_used, xs, w1, w3, w2)


def _combine_kernel(pos_first_ref, pos_next_ref, y_hbm, x_ref, route_ref, mod_ref, o_ref, ybuf, sems, *, tc):
    step = pl.program_id(0)
    slot = step % 2

    def fetch(pos_ref, to_slot):
        def start_token(t):
            for k in range(2):
                _row_copy(y_hbm, ybuf.at[to_slot, k], pos_ref[2 * t + k], t, sems.at[to_slot]).start()
        _row_dma_issue(tc, start_token)

    @pl.when(step == 0)
    def _():
        fetch(pos_first_ref, 0)

    @pl.when(step + 1 < pl.num_programs(0))
    def _():
        fetch(pos_next_ref, 1 - slot)

    _row_dma_drain(tc, lambda: _row_copy(y_hbm, ybuf.at[slot, 0], 0, 0, sems.at[slot]).wait())
    route = route_ref[...]
    f = route[:, 4:5] * ybuf[slot, 0] + route[:, 5:6] * ybuf[slot, 1]
    o_ref[...] = x_ref[...] + mod_ref[5:6, :] * f


def _combine(y, pos, x2, route, mod, seq):
    n, d = x2.shape
    tc = min(256, seq)
    last = n // tc - 1
    return pl.pallas_call(
        functools.partial(_combine_kernel, tc=tc),
        grid=(n // tc,),
        in_specs=[
            pl.BlockSpec((2 * tc,), lambda i: (0,), memory_space=pltpu.SMEM),
            pl.BlockSpec((2 * tc,), lambda i: (jnp.minimum(i + 1, last),), memory_space=pltpu.SMEM),
            pl.BlockSpec(memory_space=pl.ANY),
            pl.BlockSpec((tc, d), lambda i: (i, 0)),
            pl.BlockSpec((tc, LANES), lambda i: (i, 0)),
            pl.BlockSpec((None, ADA_CHUNKS, d), lambda i: (i * tc // seq, 0, 0)),
        ],
        out_specs=pl.BlockSpec((tc, d), lambda i: (i, 0)),
        out_shape=jax.ShapeDtypeStruct((n, d), F32),
        scratch_shapes=[pltpu.VMEM((2, 2, tc, d), F32), pltpu.SemaphoreType.DMA((2,))],
        compiler_params=_cparams(("arbitrary",)),
        name="moe_combine",
    )(pos, pos, y, x2, route, mod)


def _moe(h2, x2, mod, router, w1, w3, w2, seq):
    n, _ = x2.shape
    tm = min(512, n)
    route, counts = _route(h2, router)
    cnt = counts[0, :N_EXPERTS].astype(jnp.int32)
    padded = (cnt + tm - 1) // tm * tm
    ends = jnp.cumsum(padded)
    offs = ends - padded
    ids = route[:, 0:2].astype(jnp.int32)
    ranks = route[:, 2:4].astype(jnp.int32)
    pos = (offs[ids] + ranks).reshape(-1)
    n_tiles = 2 * n // tm + N_EXPERTS
    starts = jnp.arange(n_tiles, dtype=jnp.int32) * tm
    tile_expert = jnp.minimum(jnp.sum(starts[:, None] >= ends[None, :], axis=1), N_EXPERTS - 1).astype(jnp.int32)
    n_used = (ends[-1:] // tm).astype(jnp.int32)

    xs = _dispatch(h2, pos, n_tiles * tm)
    y = _experts(xs, tile_expert, n_used, w1, w3, w2, tm)
    return _combine(y, pos, x2, route, mod, seq)


def kernel(x, c, ada_w, ada_b, norm_mix_g, norm_ffn_g, w_in, hgrn_lb_logits, hgrn_norm_g, da_qnorm_g, da_knorm_g, da_lambda, da_subln_g, w_branch_a, w_branch_b, w_out, ffn_w1, ffn_w3, ffn_w2, moe_router, moe_w1, moe_w3, moe_w2):
    batch, seq, d = x.shape
    depth = ada_w.shape[0]
    n = batch * seq
    x2 = x.reshape(n, d)

    p = jax.nn.softmax(hgrn_lb_logits.astype(F32), axis=0)
    cum = jnp.cumsum(p, axis=0)
    lb_all = cum - cum[0:1]

    mods = _ada(c, ada_w, ada_b).reshape(depth, batch, ADA_CHUNKS, d)

    for l in range(depth):
        mod = mods[l]
        moe_layer = l % 2 == 1
        proj = _inproj(x2, mod, norm_mix_g[l], w_in[l].astype(BF16), seq)
        o_hg = _hgrn(proj, lb_all[l], hgrn_norm_g[l], batch, seq)
        o_da = _attn(proj, da_qnorm_g[l], da_knorm_g[l], da_lambda[l], da_subln_g[l], l, batch, seq)
        x2, h2 = _merge(x2, proj, o_hg, o_da, w_branch_a[l].astype(BF16), w_branch_b[l].astype(BF16),
                        w_out[l].astype(BF16), mod, norm_ffn_g[l], seq, F32 if moe_layer else BF16)
        if moe_layer:
            x2 = _moe(h2, x2, mod, moe_router[l // 2], moe_w1[l // 2].astype(BF16),
                      moe_w3[l // 2].astype(BF16), moe_w2[l // 2].astype(BF16), seq)
        else:
            x2 = _ffn(h2, x2, mod, ffn_w1[l // 2].astype(BF16), ffn_w3[l // 2].astype(BF16),
                      ffn_w2[l // 2].astype(BF16), seq)
    return x2.reshape(batch, seq, d)
```

```python
import functools
import math

import jax
import jax.numpy as jnp
from jax import lax
from jax.experimental import pallas as pl
from jax.experimental.pallas import tpu as pltpu

F32 = jnp.float32
BF16 = jnp.bfloat16
EPS = 1e-6

D_MODEL = 1024
HG_HEADS = 4
HG_D = 128
HG_WIDTH = HG_HEADS * HG_D
HG_CHUNK = 16
DA_HEADS = 4
DA_D = 64
DA_V = 2 * DA_D
DA_WIDTH = DA_HEADS * DA_V
N_EXPERTS = 8
ADA_CHUNKS = 6
LANES = 128
COL_HQ, COL_HF, COL_HI, COL_HG = 0, 512, 1024, 1536
COL_DQ, COL_DK, COL_DV = 2048, 2560, 3072
COL_GA, COL_GB = 3584, 4608
D_IN = 5632
NEG_BIG = -1e30
LOG2E = 1.4426950408889634
V_PAD = 16
ROW_DMA_UNROLL = 8
BOUND_LIMIT = 80.0
VMEM_LIMIT = 56 * 1024 * 1024


def _cparams(sem):
    return pltpu.CompilerParams(dimension_semantics=sem, vmem_limit_bytes=VMEM_LIMIT)


def _sigmoid(x):
    return 1.0 / (1.0 + jnp.exp(-x))


def _silu(x):
    return x * _sigmoid(x)


def _rms(x, gain):
    ms = jnp.mean(x * x, axis=-1, keepdims=True)
    return x * lax.rsqrt(ms + EPS) * gain


def _ada_kernel(c_ref, w_ref, b_ref, o_ref):
    cs = _silu(c_ref[...])
    o_ref[...] = jnp.dot(cs, w_ref[...], precision=lax.Precision.HIGHEST,
                         preferred_element_type=F32) + b_ref[...]


def _ada(c, ada_w, ada_b):
    depth, d, n6 = ada_w.shape
    b = c.shape[0]
    tn = 1536
    return pl.pallas_call(
        _ada_kernel,
        grid=(depth, n6 // tn),
        in_specs=[
            pl.BlockSpec((b, d), lambda l, j: (0, 0)),
            pl.BlockSpec((None, d, tn), lambda l, j: (l, 0, j)),
            pl.BlockSpec((None, 1, tn), lambda l, j: (l, 0, j)),
        ],
        out_specs=pl.BlockSpec((None, b, tn), lambda l, j: (l, 0, j)),
        out_shape=jax.ShapeDtypeStruct((depth, b, n6), F32),
        compiler_params=_cparams(("arbitrary", "arbitrary")),
        name="ada_mod",
    )(c, ada_w, ada_b.reshape(depth, 1, n6))


def _norm_halves(x, g):
    lo = lax.broadcasted_iota(jnp.int32, (1, LANES), 1) < DA_D
    sq = x * x
    s_lo = jnp.sum(jnp.where(lo, sq, 0.0), axis=-1, keepdims=True)
    s_hi = jnp.sum(jnp.where(lo, 0.0, sq), axis=-1, keepdims=True)
    return x * lax.rsqrt(jnp.where(lo, s_lo, s_hi) * (1.0 / DA_D) + EPS) * g


def _inproj_kernel(x_ref, mod_ref, g_ref, w_ref, qg_ref, kg_ref, o_ref, vt_ref):
    h = (_rms(x_ref[...], g_ref[...]) * (1.0 + mod_ref[1:2, :]) + mod_ref[0:1, :]).astype(BF16)
    o_ref[:, 0:COL_DQ] = jnp.dot(h, w_ref[:, 0:COL_DQ], preferred_element_type=F32).astype(o_ref.dtype)
    att = jnp.dot(h, w_ref[:, COL_DQ:COL_GA], preferred_element_type=F32)
    for hd in range(DA_HEADS):
        c0 = hd * DA_V
        qn = _norm_halves(att[:, c0:c0 + DA_V], qg_ref[...]) * (DA_D ** -0.5 * LOG2E)
        kn = _norm_halves(att[:, DA_WIDTH + c0:DA_WIDTH + c0 + DA_V], kg_ref[...])
        o_ref[:, COL_DQ + c0:COL_DQ + c0 + DA_V] = qn.astype(o_ref.dtype)
        o_ref[:, COL_DK + c0:COL_DK + c0 + DA_V] = kn.astype(o_ref.dtype)
    v = att[:, 2 * DA_WIDTH:3 * DA_WIDTH]
    o_ref[:, COL_DV:COL_GA] = v.astype(o_ref.dtype)
    vt_ref[...] = v.T.astype(vt_ref.dtype)
    o_ref[:, COL_GA:D_IN] = jnp.dot(h, w_ref[:, COL_GA:D_IN], preferred_element_type=F32).astype(o_ref.dtype)


def _inproj(x2, mod, gain, w_in_bf, qg, kg, seq):
    n, d = x2.shape
    tm = min(512, seq)
    vec = lambda a: jnp.concatenate([a, a]).reshape(1, DA_V)
    return pl.pallas_call(
        _inproj_kernel,
        grid=(n // tm,),
        in_specs=[
            pl.BlockSpec((tm, d), lambda i: (i, 0)),
            pl.BlockSpec((None, ADA_CHUNKS, d), lambda i: (i * tm // seq, 0, 0)),
            pl.BlockSpec((1, d), lambda i: (0, 0)),
            pl.BlockSpec((d, D_IN), lambda i: (0, 0), pipeline_mode=pl.Buffered(1)),
            pl.BlockSpec((1, DA_V), lambda i: (0, 0)),
            pl.BlockSpec((1, DA_V), lambda i: (0, 0)),
        ],
        out_specs=[pl.BlockSpec((tm, D_IN), lambda i: (i, 0)),
                   pl.BlockSpec((DA_WIDTH, tm), lambda i: (0, i))],
        out_shape=[jax.ShapeDtypeStruct((n, D_IN), BF16),
                   jax.ShapeDtypeStruct((DA_WIDTH, n), BF16)],
        compiler_params=_cparams(("arbitrary",)),
        name="norm_inproj",
    )(x2, mod, gain.reshape(1, d), w_in_bf, vec(qg), vec(kg))


def _hgrn_kernel(q_ref, z_ref, i_ref, og_ref, lb_ref, g_ref, o_ref, st_scr, *, n_chunks):
    c_rows = HG_CHUNK

    @pl.when(pl.program_id(1) == 0)
    def _():
        st_scr[...] = jnp.zeros_like(st_scr)

    half = c_rows // 2
    row = lax.broadcasted_iota(jnp.int32, (c_rows, HG_D), 0)
    row_half = lax.broadcasted_iota(jnp.int32, (half, HG_D), 0)
    gain = g_ref[...]
    lb = lb_ref[...]
    log_lb = jnp.log(lb)
    log1m_lb = jnp.log1p(-lb)
    one_m_lb = 1.0 - lb

    def body(c, carry):
        r0 = pl.multiple_of(c * c_rows, c_rows)
        for h in range(HG_HEADS):
            cs = slice(h * HG_D, (h + 1) * HG_D)
            z = z_ref[pl.ds(r0, c_rows), cs].astype(F32)
            q = q_ref[pl.ds(r0, c_rows), cs].astype(F32)
            v = i_ref[pl.ds(r0, c_rows), cs].astype(F32)
            og = og_ref[pl.ds(r0, c_rows), cs].astype(F32)

            e = jnp.exp(-jnp.abs(z))
            log_sig = jnp.minimum(z, 0.0) - jnp.log(1.0 + e)
            y = log1m_lb[:, cs] + log_sig
            a = log_lb[:, cs]
            log_f = jnp.maximum(a, y) + jnp.log(1.0 + jnp.exp(-jnp.abs(a - y)))
            sig_neg = jnp.where(z >= 0.0, e, 1.0) / (1.0 + e)
            k_in = one_m_lb[:, cs] * sig_neg

            b = log_f * LOG2E
            for d in (1, 2, 4, 8):
                b = b + jnp.where(row >= d, pltpu.roll(b, d, 0), 0.0)
            b_last = b[c_rows - 1:c_rows, :]

            st = st_scr[h]
            qe = (q * jnp.exp2(b)).astype(BF16)
            o = lax.dot_general(qe, st.astype(BF16), (((1,), (1,)), ((), ())),
                                preferred_element_type=F32)
            o_half = [o[0:half, :], o[half:c_rows, :]]
            b_mid = b[half - 1:half, :]
            q_bot = q[half:c_rows, :] * jnp.exp2(b[half:c_rows, :] - b_mid)
            k_top = k_in[0:half, :] * jnp.exp2(b_mid - b[0:half, :])
            for s in range(c_rows):
                hi = s // half
                rel = jnp.where(row_half >= s - hi * half, b[hi * half:(hi + 1) * half, :] - b[s:s + 1, :], -jnp.inf)
                w = jnp.exp2(rel) * q[hi * half:(hi + 1) * half, :] * k_in[s:s + 1, :]
                o_half[hi] = o_half[hi] + jnp.sum(w, axis=-1, keepdims=True) * v[s:s + 1, :]
                if hi == 0:
                    w = q_bot * k_top[s:s + 1, :]
                    o_half[1] = o_half[1] + jnp.sum(w, axis=-1, keepdims=True) * v[s:s + 1, :]
            o = jnp.concatenate(o_half, axis=0)

            k_dec = k_in * jnp.exp2(b_last - b)
            u = lax.dot_general(v.astype(BF16), k_dec.astype(BF16), (((0,), (0,)), ((), ())),
                                preferred_element_type=F32)
            st_scr[h] = st * jnp.exp2(b_last) + u

            out = _rms(o, gain) * _silu(og)
            o_ref[pl.ds(r0, c_rows), cs] = out.astype(o_ref.dtype)
        return carry

    lax.fori_loop(0, n_chunks, body, 0, unroll=2)


def _hgrn(proj, lb, gain, batch, seq):
    n = proj.shape[0]
    ts = min(512, seq)
    nsb = seq // ts
    wb = HG_WIDTH

    def col(cb):
        return pl.BlockSpec((ts, wb), lambda b, i: (b * nsb + i, cb))

    return pl.pallas_call(
        functools.partial(_hgrn_kernel, n_chunks=ts // HG_CHUNK),
        grid=(batch, nsb),
        in_specs=[col(COL_HQ // wb), col(COL_HF // wb), col(COL_HI // wb), col(COL_HG // wb),
                  pl.BlockSpec((1, wb), lambda b, i: (0, 0)),
                  pl.BlockSpec((1, HG_D), lambda b, i: (0, 0))],
        out_specs=pl.BlockSpec((ts, wb), lambda b, i: (b * nsb + i, 0)),
        out_shape=jax.ShapeDtypeStruct((n, wb), BF16),
        scratch_shapes=[pltpu.VMEM((HG_HEADS, HG_D, HG_D), F32)],
        compiler_params=_cparams(("arbitrary", "arbitrary")),
        name="hgrn2_scan",
    )(proj, proj, proj, proj, lb.reshape(1, wb), gain.reshape(1, HG_D))


def _split3(x):
    a = x.astype(BF16).astype(F32)
    b = (x - a).astype(BF16).astype(F32)
    c = (x - a - b).astype(BF16).astype(F32)
    return a, b, c


def _attn_kernel(slope_ref, q_ref, k_ref, vt_ref, qg_ref, kg_ref, lam_ref, sg_ref, o_ref,
                 kn_scr, vt_scr, qaug_scr, qa_scr, m0_scr, m1_scr, acc0_scr, acc1_scr,
                 *, tq, tk, seq, lam_init):
    i = pl.program_id(2)
    lane = lax.broadcasted_iota(jnp.int32, (1, LANES), 1)
    lo = lane < DA_D
    slope2 = slope_ref[...] * LOG2E

    sa, sb, sc = _split3(slope2)

    def lane_table(shape, values):
        lane_idx = lax.broadcasted_iota(jnp.int32, shape, len(shape) - 1)
        out = jnp.zeros(shape, F32)
        for col, val in enumerate(values):
            out = jnp.where(lane_idx == col, val, out)
        return out

    @pl.when(i == 0)
    def _():
        rowk = lax.broadcasted_iota(jnp.int32, (tk, LANES), 0)
        rowq = lax.broadcasted_iota(jnp.int32, (tq, LANES), 0)
        ones_row = jnp.where(lax.broadcasted_iota(jnp.int32, (V_PAD, tk), 0) == 0, 1.0, 0.0).astype(BF16)
        key_hi = (rowk >> 1).astype(F32)
        key_lo = (rowk & 1).astype(F32)
        qry_hi = (rowq >> 1).astype(F32)
        qry_lo = (rowq & 1).astype(F32)
        qaug_scr[...] = lane_table((tq, LANES), [2.0 * sa, 2.0 * sb, 2.0 * sc, sa, sb, sc,
                                                 float(tk) * sa, float(tk) * sb, float(tk) * sc, 0.0, 0.0, 0.0,
                                                 qry_hi, qry_hi, qry_hi, qry_lo, qry_lo, qry_lo])

        key_cols = lane_table((tk, LANES), [key_hi, key_hi, key_hi, key_lo, key_lo, key_lo, 0.0, 0.0, 0.0, 1.0, 1.0, 1.0,
                                            -2.0 * sa, -2.0 * sb, -2.0 * sc, -sa, -sb, -sc])
        lanek = lax.broadcasted_iota(jnp.int32, (tk, LANES), 1)
        block_lanes = jnp.logical_and(lanek >= 6, lanek < 9)

        def kb(j, carry):
            r = pl.multiple_of(j * tk, tk)
            aug = jnp.where(block_lanes, lax.convert_element_type(j, F32), key_cols)
            kn_scr[j, :, 0:LANES] = k_ref[pl.ds(r, tk), :]
            kn_scr[j, :, LANES:2 * LANES] = aug.astype(BF16)
            vt_scr[j, 0:DA_V, :] = vt_ref[:, pl.ds(r, tk)]
            vt_scr[j, DA_V:DA_V + V_PAD, :] = ones_row
            return carry

        lax.fori_loop(0, seq // tk, kb, 0)

    qn = q_ref[...]
    gq = jnp.max(jnp.abs(qg_ref[...]), axis=-1, keepdims=True)
    gk = jnp.max(jnp.abs(kg_ref[...]), axis=-1, keepdims=True)
    bound = (1.02 * DA_D * DA_D ** -0.5 * LOG2E) * gq * gk
    one_pass = bound[0, 0] * 2.0 <= BOUND_LIMIT
    use_bound = jnp.where(bound * 2.0 <= BOUND_LIMIT, 1.0, 0.0)

    ca, cb, cc = _split3(-(bound * use_bound + slope2 * (i * tq).astype(F32)))
    block_cols = lane_table((1, LANES), [0.0] * 9 + [ca, cb, cc])
    lane_q = lax.broadcasted_iota(jnp.int32, (tq, LANES), 1)
    shift_cols = jnp.where(jnp.logical_and(lane_q >= 9, lane_q < 12), block_cols, qaug_scr[...]).astype(BF16)
    qa_scr[0, :, 0:LANES] = jnp.where(lo, qn, jnp.zeros_like(qn))
    qa_scr[1, :, 0:LANES] = jnp.where(lo, jnp.zeros_like(qn), qn)
    for c in range(2):
        qa_scr[c, :, LANES:2 * LANES] = shift_cols

    accs = (acc0_scr, acc1_scr)
    maxs = (m0_scr, m1_scr)
    for c in range(2):
        accs[c][...] = jnp.zeros_like(accs[c])
        maxs[c][...] = jnp.full_like(maxs[c], NEG_BIG)

    def scores(j, masked):
        kc = kn_scr[j]
        out = []
        for c in range(2):
            s = lax.dot_general(kc, qa_scr[c], (((1,), (1,)), ((), ())), preferred_element_type=F32)
            if masked:
                keyi = lax.broadcasted_iota(jnp.int32, (tk, tq), 0)
                qryi = lax.broadcasted_iota(jnp.int32, (tk, tq), 1)
                s = jnp.where(keyi <= qryi, s, NEG_BIG)
            out.append(s)
        return out

    hk = tk // 2

    def diag_scores(j):
        out = []
        for c in range(2):
            s_top = lax.dot_general(kn_scr[j, 0:hk, :], qa_scr[c], (((1,), (1,)), ((), ())),
                                    preferred_element_type=F32)
            s_bot = lax.dot_general(kn_scr[j, hk:tk, :], qa_scr[c, hk:tq, :], (((1,), (1,)), ((), ())),
                                    preferred_element_type=F32)
            tri = lambda s: jnp.where(lax.broadcasted_iota(jnp.int32, s.shape, 0)
                                      <= lax.broadcasted_iota(jnp.int32, s.shape, 1), s, NEG_BIG)
            out.append((tri(s_top), tri(s_bot)))
        return out

    def one_pass_update(blocks):
        sc_all = [diag_scores(j) if masked else scores(j, False) for j, masked in blocks]
        for b, (j, masked) in enumerate(blocks):
            for c in range(2):
                if masked:
                    p_top, p_bot = (jnp.exp2(s).astype(BF16) for s in sc_all[b][c])
                    accs[c][...] += jnp.dot(vt_scr[j, :, 0:hk], p_top, preferred_element_type=F32)
                    accs[c][:, hk:tq] += jnp.dot(vt_scr[j, :, hk:tk], p_bot, preferred_element_type=F32)
                else:
                    p = jnp.exp2(sc_all[b][c]).astype(BF16)
                    accs[c][...] += jnp.dot(vt_scr[j], p, preferred_element_type=F32)

    def running_max_update(j, masked):
        vt = vt_scr[j]
        sc_j = scores(j, masked)
        for c in range(2):
            m_old = maxs[c][...]
            m_new = jnp.maximum(m_old, jnp.max(sc_j[c], axis=0, keepdims=True))
            p = jnp.exp2(sc_j[c] - m_new).astype(BF16)
            accs[c][...] = jnp.exp2(m_old - m_new) * accs[c][...] + jnp.dot(vt, p, preferred_element_type=F32)
            maxs[c][...] = m_new

    @pl.when(one_pass)
    def _():
        def pair(jj, carry):
            one_pass_update(((2 * jj, False), (2 * jj + 1, False)))
            return carry

        lax.fori_loop(0, i // 2, pair, 0)

        @pl.when(i % 2 == 1)
        def _():
            one_pass_update(((i - 1, False), (i, True)))

        @pl.when(i % 2 == 0)
        def _():
            one_pass_update(((i, True),))

    @pl.when(jnp.logical_not(one_pass))
    def _():
        def single(j, carry):
            running_max_update(j, False)
            return carry

        lax.fori_loop(0, i, single, 0)
        running_max_update(i, True)

    lv = lam_ref[...]
    lam = (jnp.exp(jnp.sum(lv[0:1, :] * lv[1:2, :], axis=-1, keepdims=True))
           - jnp.exp(jnp.sum(lv[2:3, :] * lv[3:4, :], axis=-1, keepdims=True)) + lam_init)
    o = (acc0_scr[0:DA_V, :] / acc0_scr[DA_V:DA_V + 1, :]
         - lam * (acc1_scr[0:DA_V, :] / acc1_scr[DA_V:DA_V + 1, :]))
    ms = jnp.mean(o * o, axis=0, keepdims=True)
    on = o * lax.rsqrt(ms + EPS) * sg_ref[...] * (1.0 - lam_init)
    o_ref[...] = on.astype(o_ref.dtype)


def _attn(proj, v_t, qg, kg, lam_vecs, sg, layer, batch, seq):
    n = proj.shape[0]
    tq = tk = min(512, seq)
    nqb = seq // tq
    lam_init = 0.8 - 0.6 * math.exp(-0.3 * layer)
    slopes = jnp.asarray(2.0 ** (-8.0 * jnp.arange(1, DA_HEADS + 1) / DA_HEADS), F32)
    slopes = jnp.broadcast_to(slopes[:, None, None], (DA_HEADS, 1, LANES))
    vec = lambda a: jnp.concatenate([a, a]).reshape(1, DA_V)
    sg_cols = jnp.broadcast_to(sg.astype(F32)[:, None], (DA_V, tq))
    v_rows = DA_V + V_PAD
    return pl.pallas_call(
        functools.partial(_attn_kernel, tq=tq, tk=tk, seq=seq, lam_init=lam_init),
        grid=(batch, DA_HEADS, nqb),
        in_specs=[
            pl.BlockSpec((None, 1, LANES), lambda b, h, i: (h, 0, 0)),
            pl.BlockSpec((tq, DA_V), lambda b, h, i: (b * nqb + i, COL_DQ // DA_V + h)),
            pl.BlockSpec((seq, DA_V), lambda b, h, i: (b, COL_DK // DA_V + h)),
            pl.BlockSpec((DA_V, seq), lambda b, h, i: (h, b)),
            pl.BlockSpec((1, DA_V), lambda b, h, i: (0, 0)),
            pl.BlockSpec((1, DA_V), lambda b, h, i: (0, 0)),
            pl.BlockSpec((4, DA_D), lambda b, h, i: (0, 0)),
            pl.BlockSpec((DA_V, tq), lambda b, h, i: (0, 0)),
        ],
        out_specs=pl.BlockSpec((DA_V, tq), lambda b, h, i: (h, b * nqb + i)),
        out_shape=jax.ShapeDtypeStruct((DA_WIDTH, n), BF16),
        scratch_shapes=[
            pltpu.VMEM((seq // tk, tk, 2 * LANES), BF16),
            pltpu.VMEM((seq // tk, v_rows, tk), BF16),
            pltpu.VMEM((tq, LANES), F32),
            pltpu.VMEM((2, tq, 2 * LANES), BF16),
            pltpu.VMEM((1, tq), F32), pltpu.VMEM((1, tq), F32),
            pltpu.VMEM((v_rows, tq), F32), pltpu.VMEM((v_rows, tq), F32),
        ],
        compiler_params=_cparams(("arbitrary", "arbitrary", "arbitrary")),
        name="diff_attn",
    )(slopes, proj, proj, v_t, vec(qg), vec(kg), lam_vecs, sg_cols)


def _merge_kernel(x_ref, ga0_ref, ga1_ref, gb0_ref, gb1_ref, ohg_ref, oda_ref, wpa_ref, wpb_ref, wo_ref,
                  mod_ref, gf_ref, xo_ref, h2_ref):
    a = jnp.dot(ohg_ref[...], wpa_ref[...], preferred_element_type=F32)
    b = lax.dot_general(oda_ref[...], wpb_ref[...], (((0,), (0,)), ((), ())),
                        preferred_element_type=F32)
    ga = jnp.concatenate([ga0_ref[...], ga1_ref[...]], axis=1).astype(F32)
    gb = jnp.concatenate([gb0_ref[...], gb1_ref[...]], axis=1).astype(F32)
    y = _sigmoid(ga) * a + _sigmoid(gb) * b
    mix = jnp.dot(y.astype(BF16), wo_ref[...], preferred_element_type=F32)
    xn = x_ref[...] + mod_ref[2:3, :] * mix
    xo_ref[...] = xn
    h = _rms(xn, gf_ref[...]) * (1.0 + mod_ref[4:5, :]) + mod_ref[3:4, :]
    h2_ref[...] = h.astype(h2_ref.dtype)


def _merge(x2, proj, o_hg, o_da, wpa, wpb, wo, mod, gain_ffn, seq, h2_dtype):
    n, d = x2.shape
    tm = min(512, seq)
    row = lambda w: pl.BlockSpec((tm, w), lambda i: (i, 0))
    full = lambda a: pl.BlockSpec(a.shape, lambda i: (0, 0))
    half = d // 2
    gate = lambda cb: pl.BlockSpec((tm, half), lambda i: (i, cb))
    return pl.pallas_call(
        _merge_kernel,
        grid=(n // tm,),
        in_specs=[
            row(d),
            gate(COL_GA // half), gate(COL_GA // half + 1),
            gate(COL_GB // half), gate(COL_GB // half + 1),
            row(HG_WIDTH), pl.BlockSpec((DA_WIDTH, tm), lambda i: (0, i)),
            full(wpa), full(wpb), full(wo),
            pl.BlockSpec((None, ADA_CHUNKS, d), lambda i: (i * tm // seq, 0, 0)),
            pl.BlockSpec((1, d), lambda i: (0, 0)),
        ],
        out_specs=[row(d), row(d)],
        out_shape=[jax.ShapeDtypeStruct((n, d), F32), jax.ShapeDtypeStruct((n, d), h2_dtype)],
        compiler_params=_cparams(("arbitrary",)),
        name="merge_outproj",
    )(x2, proj, proj, proj, proj, o_hg, o_da, wpa, wpb, wo, mod, gain_ffn.reshape(1, d))


def _ffn_kernel(h_ref, x_ref, mod_ref, w1_ref, w3_ref, w2_ref, o_ref, acc_scr):
    f = pl.program_id(1)

    @pl.when(f == 0)
    def _():
        acc_scr[...] = jnp.zeros_like(acc_scr)

    h = h_ref[...]
    g = _silu(jnp.dot(h, w1_ref[...], preferred_element_type=F32)) * jnp.dot(
        h, w3_ref[...], preferred_element_type=F32)
    acc_scr[...] += jnp.dot(g.astype(BF16), w2_ref[...], preferred_element_type=F32)

    @pl.when(f == pl.num_programs(1) - 1)
    def _():
        o_ref[...] = x_ref[...] + mod_ref[5:6, :] * acc_scr[...]


def _ffn(h2, x2, mod, w1, w3, w2, seq):
    n, d = x2.shape
    dff = w1.shape[1]
    tm = min(512, seq)
    tf = dff
    once = pl.Buffered(1)
    return pl.pallas_call(
        _ffn_kernel,
        grid=(n // tm, dff // tf),
        in_specs=[
            pl.BlockSpec((tm, d), lambda i, f: (i, 0)),
            pl.BlockSpec((tm, d), lambda i, f: (i, 0)),
            pl.BlockSpec((None, ADA_CHUNKS, d), lambda i, f: (i * tm // seq, 0, 0)),
            pl.BlockSpec((d, tf), lambda i, f: (0, f), pipeline_mode=once),
            pl.BlockSpec((d, tf), lambda i, f: (0, f), pipeline_mode=once),
            pl.BlockSpec((tf, d), lambda i, f: (f, 0), pipeline_mode=once),
        ],
        out_specs=pl.BlockSpec((tm, d), lambda i, f: (i, 0)),
        out_shape=jax.ShapeDtypeStruct((n, d), F32),
        scratch_shapes=[pltpu.VMEM((tm, d), F32)],
        compiler_params=_cparams(("arbitrary", "arbitrary")),
        name="dense_swiglu",
    )(h2, x2, mod, w1, w3, w2)


def _route_kernel(h_ref, r_ref, route_ref, cnt_ref, carry_scr):
    tm = h_ref.shape[0]

    @pl.when(pl.program_id(0) == 0)
    def _():
        carry_scr[...] = jnp.zeros_like(carry_scr)

    h = h_ref[...]
    h_hi = h.astype(BF16)
    h_lo = (h - h_hi.astype(F32)).astype(BF16)
    r = r_ref[...]
    r_hi = r.astype(BF16)
    r_lo = (r - r_hi.astype(F32)).astype(BF16)
    logits = (jnp.dot(h_hi, r_hi, preferred_element_type=F32) + jnp.dot(h_hi, r_lo, preferred_element_type=F32)
              + jnp.dot(h_lo, r_hi, preferred_element_type=F32))
    lane = lax.broadcasted_iota(jnp.int32, (tm, LANES), 1).astype(F32)
    lg = jnp.where(lane < N_EXPERTS, logits, -jnp.inf)
    m0 = jnp.max(lg, axis=-1, keepdims=True)
    i0 = jnp.min(jnp.where(lg == m0, lane, float(LANES)), axis=-1, keepdims=True)
    lg1 = jnp.where(lane == i0, -jnp.inf, lg)
    m1 = jnp.max(lg1, axis=-1, keepdims=True)
    i1 = jnp.min(jnp.where(lg1 == m1, lane, float(LANES)), axis=-1, keepdims=True)
    e = jnp.exp(m1 - m0)
    w0 = 1.0 / (1.0 + e)
    w1 = e / (1.0 + e)

    sel = jnp.where(lane == i0, 1.0, jnp.where(lane == i1, 1.0, 0.0))
    rr = lax.broadcasted_iota(jnp.int32, (tm, tm), 0)
    cc = lax.broadcasted_iota(jnp.int32, (tm, tm), 1)
    tri = jnp.where(rr > cc, 1.0, 0.0).astype(BF16)
    before = jnp.dot(tri, sel.astype(BF16), preferred_element_type=F32) + carry_scr[...]
    r0 = jnp.sum(jnp.where(lane == i0, before, 0.0), axis=-1, keepdims=True)
    r1 = jnp.sum(jnp.where(lane == i1, before, 0.0), axis=-1, keepdims=True)
    carry_scr[...] += jnp.sum(sel, axis=0, keepdims=True)

    route = jnp.where(lane == 0.0, i0, jnp.where(lane == 1.0, i1, jnp.where(
        lane == 2.0, r0, jnp.where(lane == 3.0, r1, jnp.where(
            lane == 4.0, w0, jnp.where(lane == 5.0, w1, 0.0))))))
    route_ref[...] = route
    cnt_ref[...] = carry_scr[...]


def _route(h2, router):
    n, d = h2.shape
    tm = min(512, n)
    r_pad = jnp.zeros((d, LANES), F32).at[:, :N_EXPERTS].set(router)
    return pl.pallas_call(
        _route_kernel,
        grid=(n // tm,),
        in_specs=[pl.BlockSpec((tm, d), lambda i: (i, 0)),
                  pl.BlockSpec((d, LANES), lambda i: (0, 0))],
        out_specs=[pl.BlockSpec((tm, LANES), lambda i: (i, 0)),
                   pl.BlockSpec((1, LANES), lambda i: (0, 0))],
        out_shape=[jax.ShapeDtypeStruct((n, LANES), F32), jax.ShapeDtypeStruct((1, LANES), F32)],
        scratch_shapes=[pltpu.VMEM((1, LANES), F32)],
        compiler_params=_cparams(("arbitrary",)),
        name="moe_route",
    )(h2, r_pad)


def _row_copy(src_hbm, dst_ref, src_row, dst_row, sem):
    return pltpu.make_async_copy(src_hbm.at[pl.ds(src_row, 1)], dst_ref.at[pl.ds(dst_row, 1)], sem)


def _row_dma_issue(n_tokens, start_token):
    def issue(g, c):
        for u in range(ROW_DMA_UNROLL):
            start_token(g * ROW_DMA_UNROLL + u)
        return c

    lax.fori_loop(0, n_tokens // ROW_DMA_UNROLL, issue, 0)


def _row_dma_drain(n_tokens, wait_row):
    def drain(g, c):
        for _ in range(2 * ROW_DMA_UNROLL):
            wait_row()
        return c

    lax.fori_loop(0, n_tokens // ROW_DMA_UNROLL, drain, 0)


def _dispatch_kernel(pos_ref, h_ref, xs_init_hbm, xs_hbm, sem, *, tg):
    del xs_init_hbm

    def start_token(t):
        for k in range(2):
            _row_copy(h_ref, xs_hbm, t, pos_ref[2 * t + k], sem).start()

    _row_dma_issue(tg, start_token)
    _row_dma_drain(tg, lambda: _row_copy(h_ref, xs_hbm, 0, 0, sem).wait())


def _dispatch(h2, pos, n_rows):
    n, d = h2.shape
    tg = min(512, n)
    xs0 = jnp.zeros((n_rows, d), h2.dtype)
    return pl.pallas_call(
        functools.partial(_dispatch_kernel, tg=tg),
        grid=(n // tg,),
        in_specs=[pl.BlockSpec((2 * tg,), lambda i: (i,), memory_space=pltpu.SMEM),
                  pl.BlockSpec((tg, d), lambda i: (i, 0)),
                  pl.BlockSpec(memory_space=pl.ANY)],
        out_specs=pl.BlockSpec(memory_space=pl.ANY),
        out_shape=jax.ShapeDtypeStruct((n_rows, d), h2.dtype),
        scratch_shapes=[pltpu.SemaphoreType.DMA(())],
        input_output_aliases={2: 0},
        compiler_params=_cparams(("arbitrary",)),
        name="moe_dispatch",
    )(pos, h2, xs0)


def _experts_kernel(te_ref, nu_ref, x_ref, w1_ref, w3_ref, w2_ref, y_ref, xb_scr, acc_scr):
    del te_ref
    i = pl.program_id(0)
    f = pl.program_id(1)
    used = i < nu_ref[0]

    @pl.when(jnp.logical_and(used, f == 0))
    def _():
        xb_scr[...] = x_ref[...].astype(BF16)
        acc_scr[...] = jnp.zeros_like(acc_scr)

    @pl.when(used)
    def _():
        xb = xb_scr[...]
        g = _silu(jnp.dot(xb, w1_ref[...], preferred_element_type=F32)) * jnp.dot(
            xb, w3_ref[...], preferred_element_type=F32)
        acc_scr[...] += jnp.dot(g.astype(BF16), w2_ref[...], preferred_element_type=F32)

    @pl.when(jnp.logical_and(used, f == pl.num_programs(1) - 1))
    def _():
        y_ref[...] = acc_scr[...]

    @pl.when(jnp.logical_and(jnp.logical_not(used), f == 0))
    def _():
        y_ref[...] = jnp.zeros_like(y_ref)


def _experts(xs, tile_expert, n_used, w1, w3, w2, tm):
    n_rows, d = xs.shape
    dff = w1.shape[2]
    tf = dff // 2
    nf = dff // tf
    n_tiles = n_rows // tm

    def wmap(which):
        def index_map(i, f, te, nu):
            ii = jnp.maximum(jnp.minimum(i, nu[0] - 1), 0)
            ff = jnp.where(i < nu[0], f, nf - 1)
            return (te[ii], 0, ff) if which == 0 else (te[ii], ff, 0)
        return index_map

    xmap = lambda i, f, te, nu: (jnp.maximum(jnp.minimum(i, nu[0] - 1), 0), 0)
    return pl.pallas_call(
        _experts_kernel,
        grid_spec=pltpu.PrefetchScalarGridSpec(
            num_scalar_prefetch=2,
            grid=(n_tiles, nf),
            in_specs=[
                pl.BlockSpec((tm, d), xmap),
                pl.BlockSpec((None, d, tf), wmap(0)),
                pl.BlockSpec((None, d, tf), wmap(0)),
                pl.BlockSpec((None, tf, d), wmap(1)),
            ],
            out_specs=pl.BlockSpec((tm, d), lambda i, f, te, nu: (i, 0)),
            scratch_shapes=[pltpu.VMEM((tm, d), BF16), pltpu.VMEM((tm, d), F32)],
        ),
        out_shape=jax.ShapeDtypeStruct((n_rows, d), F32),
        compiler_params=_cparams(("arbitrary", "arbitrary")),
        name="moe_experts",
    )(tile_expert, n_used, xs, w1, w3, w2)


def _combine_kernel(pos_first_ref, pos_next_ref, y_hbm, x_ref, route_ref, mod_ref, o_ref, ybuf, sems, *, tc):
    step = pl.program_id(0)
    slot = step % 2

    def fetch(pos_ref, to_slot):
        def start_token(t):
            for k in range(2):
                _row_copy(y_hbm, ybuf.at[to_slot, k], pos_ref[2 * t + k], t, sems.at[to_slot]).start()
        _row_dma_issue(tc, start_token)

    @pl.when(step == 0)
    def _():
        fetch(pos_first_ref, 0)

    @pl.when(step + 1 < pl.num_programs(0))
    def _():
        fetch(pos_next_ref, 1 - slot)

    _row_dma_drain(tc, lambda: _row_copy(y_hbm, ybuf.at[slot, 0], 0, 0, sems.at[slot]).wait())
    route = route_ref[...]
    f = route[:, 4:5] * ybuf[slot, 0] + route[:, 5:6] * ybuf[slot, 1]
    o_ref[...] = x_ref[...] + mod_ref[5:6, :] * f


def _combine(y, pos, x2, route, mod, seq):
    n, d = x2.shape
    tc = min(256, seq)
    last = n // tc - 1
    return pl.pallas_call(
        functools.partial(_combine_kernel, tc=tc),
        grid=(n // tc,),
        in_specs=[
            pl.BlockSpec((2 * tc,), lambda i: (0,), memory_space=pltpu.SMEM),
            pl.BlockSpec((2 * tc,), lambda i: (jnp.minimum(i + 1, last),), memory_space=pltpu.SMEM),
            pl.BlockSpec(memory_space=pl.ANY),
            pl.BlockSpec((tc, d), lambda i: (i, 0)),
            pl.BlockSpec((tc, LANES), lambda i: (i, 0)),
            pl.BlockSpec((None, ADA_CHUNKS, d), lambda i: (i * tc // seq, 0, 0)),
        ],
        out_specs=pl.BlockSpec((tc, d), lambda i: (i, 0)),
        out_shape=jax.ShapeDtypeStruct((n, d), F32),
        scratch_shapes=[pltpu.VMEM((2, 2, tc, d), F32), pltpu.SemaphoreType.DMA((2,))],
        compiler_params=_cparams(("arbitrary",)),
        name="moe_combine",
    )(pos, pos, y, x2, route, mod)


def _moe(h2, x2, mod, router, w1, w3, w2, seq):
    n, _ = x2.shape
    tm = min(512, n)
    route, counts = _route(h2, router)
    cnt = counts[0, :N_EXPERTS].astype(jnp.int32)
    padded = (cnt + tm - 1) // tm * tm
    ends = jnp.cumsum(padded)
    offs = ends - padded
    ids = route[:, 0:2].astype(jnp.int32)
    ranks = route[:, 2:4].astype(jnp.int32)
    pos = (offs[ids] + ranks).reshape(-1)
    n_tiles = 2 * n // tm + N_EXPERTS
    starts = jnp.arange(n_tiles, dtype=jnp.int32) * tm
    tile_expert = jnp.minimum(jnp.sum(starts[:, None] >= ends[None, :], axis=1), N_EXPERTS - 1).astype(jnp.int32)
    n_used = (ends[-1:] // tm).astype(jnp.int32)

    xs = _dispatch(h2, pos, n_tiles * tm)
    y = _experts(xs, tile_expert, n_used, w1, w3, w2, tm)
    return _combine(y, pos, x2, route, mod, seq)


def kernel(x, c, ada_w, ada_b, norm_mix_g, norm_ffn_g, w_in, hgrn_lb_logits, hgrn_norm_g, da_qnorm_g, da_knorm_g, da_lambda, da_subln_g, w_branch_a, w_branch_b, w_out, ffn_w1, ffn_w3, ffn_w2, moe_router, moe_w1, moe_w3, moe_w2):
    batch, seq, d = x.shape
    depth = ada_w.shape[0]
    n = batch * seq
    x2 = x.reshape(n, d)

    p = jax.nn.softmax(hgrn_lb_logits.astype(F32), axis=0)
    cum = jnp.cumsum(p, axis=0)
    lb_all = cum - cum[0:1]

    mods = _ada(c, ada_w, ada_b).reshape(depth, batch, ADA_CHUNKS, d)

    for l in range(depth):
        mod = mods[l]
        moe_layer = l % 2 == 1
        proj, v_t = _inproj(x2, mod, norm_mix_g[l], w_in[l].astype(BF16), da_qnorm_g[l], da_knorm_g[l], seq)
        o_hg = _hgrn(proj, lb_all[l], hgrn_norm_g[l], batch, seq)
        o_da = _attn(proj, v_t, da_qnorm_g[l], da_knorm_g[l], da_lambda[l], da_subln_g[l], l, batch, seq)
        x2, h2 = _merge(x2, proj, o_hg, o_da, w_branch_a[l].astype(BF16), w_branch_b[l].astype(BF16),
                        w_out[l].astype(BF16), mod, norm_ffn_g[l], seq, F32 if moe_layer else BF16)
        if moe_layer:
            x2 = _moe(h2, x2, mod, moe_router[l // 2], moe_w1[l // 2].astype(BF16),
                      moe_w3[l // 2].astype(BF16), moe_w2[l // 2].astype(BF16), seq)
        else:
            x2 = _ffn(h2, x2, mod, ffn_w1[l // 2].astype(BF16), ffn_w3[l // 2].astype(BF16),
                      ffn_w2[l // 2].astype(BF16), seq)
    return x2.reshape(batch, seq, d)
```

```python
import functools
import math

import jax
import jax.numpy as jnp
from jax import lax
from jax.experimental import pallas as pl
from jax.experimental.pallas import tpu as pltpu

F32 = jnp.float32
BF16 = jnp.bfloat16
EPS = 1e-6

D_MODEL = 1024
HG_HEADS = 4
HG_D = 128
HG_WIDTH = HG_HEADS * HG_D
HG_CHUNK = 16
DA_HEADS = 4
DA_D = 64
DA_V = 2 * DA_D
DA_WIDTH = DA_HEADS * DA_V
N_EXPERTS = 8
ADA_CHUNKS = 6
LANES = 128
COL_HQ, COL_HF, COL_HI, COL_HG = 0, 512, 1024, 1536
COL_DQ, COL_DK, COL_DV = 2048, 2560, 3072
COL_GA, COL_GB = 3584, 4608
D_IN = 5632
NEG_BIG = -1e30
LOG2E = 1.4426950408889634
V_PAD = 16
ROW_DMA_UNROLL = 8
BOUND_LIMIT = 80.0
VMEM_LIMIT = 56 * 1024 * 1024


def _cparams(sem):
    return pltpu.CompilerParams(dimension_semantics=sem, vmem_limit_bytes=VMEM_LIMIT)


def _sigmoid(x):
    return 1.0 / (1.0 + jnp.exp(-x))


def _silu(x):
    return x * _sigmoid(x)


def _rms(x, gain):
    ms = jnp.mean(x * x, axis=-1, keepdims=True)
    return x * lax.rsqrt(ms + EPS) * gain


def _ada_kernel(c_ref, w_ref, b_ref, o_ref):
    cs = _silu(c_ref[...])
    o_ref[...] = jnp.dot(cs, w_ref[...], precision=lax.Precision.HIGHEST,
                         preferred_element_type=F32) + b_ref[...]


def _ada(c, ada_w, ada_b):
    depth, d, n6 = ada_w.shape
    b = c.shape[0]
    tn = 1536
    return pl.pallas_call(
        _ada_kernel,
        grid=(depth, n6 // tn),
        in_specs=[
            pl.BlockSpec((b, d), lambda l, j: (0, 0)),
            pl.BlockSpec((None, d, tn), lambda l, j: (l, 0, j)),
            pl.BlockSpec((None, 1, tn), lambda l, j: (l, 0, j)),
        ],
        out_specs=pl.BlockSpec((None, b, tn), lambda l, j: (l, 0, j)),
        out_shape=jax.ShapeDtypeStruct((depth, b, n6), F32),
        compiler_params=_cparams(("arbitrary", "arbitrary")),
        name="ada_mod",
    )(c, ada_w, ada_b.reshape(depth, 1, n6))


def _norm_halves(x, g):
    lo = lax.broadcasted_iota(jnp.int32, (1, LANES), 1) < DA_D
    sq = x * x
    s_lo = jnp.sum(jnp.where(lo, sq, 0.0), axis=-1, keepdims=True)
    s_hi = jnp.sum(jnp.where(lo, 0.0, sq), axis=-1, keepdims=True)
    return x * lax.rsqrt(jnp.where(lo, s_lo, s_hi) * (1.0 / DA_D) + EPS) * g


def _inproj_kernel(x_ref, mod_ref, g_ref, w_ref, qg_ref, kg_ref, o_ref, vt_ref):
    h = (_rms(x_ref[...], g_ref[...]) * (1.0 + mod_ref[1:2, :]) + mod_ref[0:1, :]).astype(BF16)
    o_ref[:, 0:COL_DQ] = jnp.dot(h, w_ref[:, 0:COL_DQ], preferred_element_type=F32).astype(o_ref.dtype)
    att = jnp.dot(h, w_ref[:, COL_DQ:COL_GA], preferred_element_type=F32)
    for hd in range(DA_HEADS):
        c0 = hd * DA_V
        qn = _norm_halves(att[:, c0:c0 + DA_V], qg_ref[...]) * (DA_D ** -0.5 * LOG2E)
        kn = _norm_halves(att[:, DA_WIDTH + c0:DA_WIDTH + c0 + DA_V], kg_ref[...])
        o_ref[:, COL_DQ + c0:COL_DQ + c0 + DA_V] = qn.astype(o_ref.dtype)
        o_ref[:, COL_DK + c0:COL_DK + c0 + DA_V] = kn.astype(o_ref.dtype)
    v = att[:, 2 * DA_WIDTH:3 * DA_WIDTH]
    o_ref[:, COL_DV:COL_GA] = v.astype(o_ref.dtype)
    vt_ref[...] = v.T.astype(vt_ref.dtype)
    o_ref[:, COL_GA:D_IN] = jnp.dot(h, w_ref[:, COL_GA:D_IN], preferred_element_type=F32).astype(o_ref.dtype)


def _inproj(x2, mod, gain, w_in_bf, qg, kg, seq):
    n, d = x2.shape
    tm = min(512, seq)
    vec = lambda a: jnp.concatenate([a, a]).reshape(1, DA_V)
    return pl.pallas_call(
        _inproj_kernel,
        grid=(n // tm,),
        in_specs=[
            pl.BlockSpec((tm, d), lambda i: (i, 0)),
            pl.BlockSpec((None, ADA_CHUNKS, d), lambda i: (i * tm // seq, 0, 0)),
            pl.BlockSpec((1, d), lambda i: (0, 0)),
            pl.BlockSpec((d, D_IN), lambda i: (0, 0), pipeline_mode=pl.Buffered(1)),
            pl.BlockSpec((1, DA_V), lambda i: (0, 0)),
            pl.BlockSpec((1, DA_V), lambda i: (0, 0)),
        ],
        out_specs=[pl.BlockSpec((tm, D_IN), lambda i: (i, 0)),
                   pl.BlockSpec((DA_WIDTH, tm), lambda i: (0, i))],
        out_shape=[jax.ShapeDtypeStruct((n, D_IN), BF16),
                   jax.ShapeDtypeStruct((DA_WIDTH, n), BF16)],
        compiler_params=_cparams(("arbitrary",)),
        name="norm_inproj",
    )(x2, mod, gain.reshape(1, d), w_in_bf, vec(qg), vec(kg))


def _hgrn_kernel(q_ref, z_ref, i_ref, og_ref, lb_ref, g_ref, o_ref, st_scr, *, n_chunks):
    c_rows = HG_CHUNK

    @pl.when(pl.program_id(1) == 0)
    def _():
        st_scr[...] = jnp.zeros_like(st_scr)

    half = c_rows // 2
    row = lax.broadcasted_iota(jnp.int32, (c_rows, HG_D), 0)
    row_half = lax.broadcasted_iota(jnp.int32, (half, HG_D), 0)
    gain = g_ref[...]
    lb = lb_ref[...]
    log_lb = jnp.log(lb)
    log1m_lb = jnp.log1p(-lb)
    one_m_lb = 1.0 - lb

    def body(c, carry):
        r0 = pl.multiple_of(c * c_rows, c_rows)
        for h in range(HG_HEADS):
            cs = slice(h * HG_D, (h + 1) * HG_D)
            z = z_ref[pl.ds(r0, c_rows), cs].astype(F32)
            q = q_ref[pl.ds(r0, c_rows), cs].astype(F32)
            v = i_ref[pl.ds(r0, c_rows), cs].astype(F32)
            og = og_ref[pl.ds(r0, c_rows), cs].astype(F32)

            e = jnp.exp(-jnp.abs(z))
            log_sig = jnp.minimum(z, 0.0) - jnp.log(1.0 + e)
            y = log1m_lb[:, cs] + log_sig
            a = log_lb[:, cs]
            log_f = jnp.maximum(a, y) + jnp.log(1.0 + jnp.exp(-jnp.abs(a - y)))
            sig_neg = jnp.where(z >= 0.0, e, 1.0) / (1.0 + e)
            k_in = one_m_lb[:, cs] * sig_neg

            b = log_f * LOG2E
            for d in (1, 2, 4, 8):
                b = b + jnp.where(row >= d, pltpu.roll(b, d, 0), 0.0)
            b_last = b[c_rows - 1:c_rows, :]

            st = st_scr[h]
            qe = (q * jnp.exp2(b)).astype(BF16)
            o = lax.dot_general(qe, st.astype(BF16), (((1,), (1,)), ((), ())),
                                preferred_element_type=F32)
            o_half = [o[0:half, :], o[half:c_rows, :]]
            b_mid = b[half - 1:half, :]
            q_bot = q[half:c_rows, :] * jnp.exp2(b[half:c_rows, :] - b_mid)
            k_top = k_in[0:half, :] * jnp.exp2(b_mid - b[0:half, :])
            for s in range(c_rows):
                hi = s // half
                rel = jnp.where(row_half >= s - hi * half, b[hi * half:(hi + 1) * half, :] - b[s:s + 1, :], -jnp.inf)
                w = jnp.exp2(rel) * q[hi * half:(hi + 1) * half, :] * k_in[s:s + 1, :]
                o_half[hi] = o_half[hi] + jnp.sum(w, axis=-1, keepdims=True) * v[s:s + 1, :]
                if hi == 0:
                    w = q_bot * k_top[s:s + 1, :]
                    o_half[1] = o_half[1] + jnp.sum(w, axis=-1, keepdims=True) * v[s:s + 1, :]
            o = jnp.concatenate(o_half, axis=0)

            k_dec = k_in * jnp.exp2(b_last - b)
            u = lax.dot_general(v.astype(BF16), k_dec.astype(BF16), (((0,), (0,)), ((), ())),
                                preferred_element_type=F32)
            st_scr[h] = st * jnp.exp2(b_last) + u

            out = _rms(o, gain) * _silu(og)
            o_ref[pl.ds(r0, c_rows), cs] = out.astype(o_ref.dtype)
        return carry

    lax.fori_loop(0, n_chunks, body, 0, unroll=2)


def _hgrn(proj, lb, gain, batch, seq):
    n = proj.shape[0]
    ts = min(512, seq)
    nsb = seq // ts
    wb = HG_WIDTH

    def col(cb):
        return pl.BlockSpec((ts, wb), lambda b, i: (b * nsb + i, cb))

    return pl.pallas_call(
        functools.partial(_hgrn_kernel, n_chunks=ts // HG_CHUNK),
        grid=(batch, nsb),
        in_specs=[col(COL_HQ // wb), col(COL_HF // wb), col(COL_HI // wb), col(COL_HG // wb),
                  pl.BlockSpec((1, wb), lambda b, i: (0, 0)),
                  pl.BlockSpec((1, HG_D), lambda b, i: (0, 0))],
        out_specs=pl.BlockSpec((ts, wb), lambda b, i: (b * nsb + i, 0)),
        out_shape=jax.ShapeDtypeStruct((n, wb), BF16),
        scratch_shapes=[pltpu.VMEM((HG_HEADS, HG_D, HG_D), F32)],
        compiler_params=_cparams(("arbitrary", "arbitrary")),
        name="hgrn2_scan",
    )(proj, proj, proj, proj, lb.reshape(1, wb), gain.reshape(1, HG_D))


def _split3(x):
    a = x.astype(BF16).astype(F32)
    b = (x - a).astype(BF16).astype(F32)
    c = (x - a - b).astype(BF16).astype(F32)
    return a, b, c


def _attn_kernel(slope_ref, q_ref, k_ref, vt_ref, qg_ref, kg_ref, lam_ref, sg_ref, o_ref,
                 kn_scr, vt_scr, qaug_scr, qa_scr, m0_scr, m1_scr, acc0_scr, acc1_scr,
                 *, tq, tk, seq, lam_init):
    i = pl.program_id(2)
    lane = lax.broadcasted_iota(jnp.int32, (1, LANES), 1)
    lo = lane < DA_D
    slope2 = slope_ref[...] * LOG2E

    sa, sb, sc = _split3(slope2)

    def lane_table(shape, values):
        lane_idx = lax.broadcasted_iota(jnp.int32, shape, len(shape) - 1)
        out = jnp.zeros(shape, F32)
        for col, val in enumerate(values):
            out = jnp.where(lane_idx == col, val, out)
        return out

    @pl.when(i == 0)
    def _():
        rowk = lax.broadcasted_iota(jnp.int32, (tk, LANES), 0)
        rowq = lax.broadcasted_iota(jnp.int32, (tq, LANES), 0)
        ones_row = jnp.where(lax.broadcasted_iota(jnp.int32, (V_PAD, tk), 0) == 0, 1.0, 0.0).astype(BF16)
        key_hi = (rowk >> 1).astype(F32)
        key_lo = (rowk & 1).astype(F32)
        qry_hi = (rowq >> 1).astype(F32)
        qry_lo = (rowq & 1).astype(F32)
        qaug_scr[...] = lane_table((tq, LANES), [2.0 * sa, 2.0 * sb, 2.0 * sc, sa, sb, sc,
                                                 float(tk) * sa, float(tk) * sb, float(tk) * sc, 0.0, 0.0, 0.0,
                                                 qry_hi, qry_hi, qry_hi, qry_lo, qry_lo, qry_lo])

        key_cols = lane_table((tk, LANES), [key_hi, key_hi, key_hi, key_lo, key_lo, key_lo, 0.0, 0.0, 0.0, 1.0, 1.0, 1.0,
                                            -2.0 * sa, -2.0 * sb, -2.0 * sc, -sa, -sb, -sc])
        lanek = lax.broadcasted_iota(jnp.int32, (tk, LANES), 1)
        block_lanes = jnp.logical_and(lanek >= 6, lanek < 9)

        def kb(j, carry):
            r = pl.multiple_of(j * tk, tk)
            aug = jnp.where(block_lanes, lax.convert_element_type(j, F32), key_cols)
            kn_scr[j, :, 0:LANES] = k_ref[pl.ds(r, tk), :]
            kn_scr[j, :, LANES:2 * LANES] = aug.astype(BF16)
            vt_scr[j, 0:DA_V, :] = vt_ref[:, pl.ds(r, tk)]
            vt_scr[j, DA_V:DA_V + V_PAD, :] = ones_row
            return carry

        lax.fori_loop(0, seq // tk, kb, 0)

    qn = q_ref[...]
    gq = jnp.max(jnp.abs(qg_ref[...]), axis=-1, keepdims=True)
    gk = jnp.max(jnp.abs(kg_ref[...]), axis=-1, keepdims=True)
    bound = (1.02 * DA_D * DA_D ** -0.5 * LOG2E) * gq * gk
    one_pass = bound[0, 0] * 2.0 <= BOUND_LIMIT
    use_bound = jnp.where(bound * 2.0 <= BOUND_LIMIT, 1.0, 0.0)

    ca, cb, cc = _split3(-(bound * use_bound + slope2 * (i * tq).astype(F32)))
    block_cols = lane_table((1, LANES), [0.0] * 9 + [ca, cb, cc])
    lane_q = lax.broadcasted_iota(jnp.int32, (tq, LANES), 1)
    shift_cols = jnp.where(jnp.logical_and(lane_q >= 9, lane_q < 12), block_cols, qaug_scr[...]).astype(BF16)
    qa_scr[0, :, 0:LANES] = jnp.where(lo, qn, jnp.zeros_like(qn))
    qa_scr[1, :, 0:LANES] = jnp.where(lo, jnp.zeros_like(qn), qn)
    for c in range(2):
        qa_scr[c, :, LANES:2 * LANES] = shift_cols

    accs = (acc0_scr, acc1_scr)
    maxs = (m0_scr, m1_scr)
    for c in range(2):
        accs[c][...] = jnp.zeros_like(accs[c])
        maxs[c][...] = jnp.full_like(maxs[c], NEG_BIG)

    def scores(j, masked):
        kc = kn_scr[j]
        out = []
        for c in range(2):
            s = lax.dot_general(kc, qa_scr[c], (((1,), (1,)), ((), ())), preferred_element_type=F32)
            if masked:
                keyi = lax.broadcasted_iota(jnp.int32, (tk, tq), 0)
                qryi = lax.broadcasted_iota(jnp.int32, (tk, tq), 1)
                s = jnp.where(keyi <= qryi, s, NEG_BIG)
            out.append(s)
        return out

    hk = tk // 2

    def diag_scores(j):
        out = []
        for c in range(2):
            s_top = lax.dot_general(kn_scr[j, 0:hk, :], qa_scr[c], (((1,), (1,)), ((), ())),
                                    preferred_element_type=F32)
            s_bot = lax.dot_general(kn_scr[j, hk:tk, :], qa_scr[c, hk:tq, :], (((1,), (1,)), ((), ())),
                                    preferred_element_type=F32)
            tri = lambda s: jnp.where(lax.broadcasted_iota(jnp.int32, s.shape, 0)
                                      <= lax.broadcasted_iota(jnp.int32, s.shape, 1), s, NEG_BIG)
            out.append((tri(s_top), tri(s_bot)))
        return out

    def one_pass_update(blocks):
        sc_all = [diag_scores(j) if masked else scores(j, False) for j, masked in blocks]
        for b, (j, masked) in enumerate(blocks):
            for c in range(2):
                if masked:
                    p_top, p_bot = (jnp.exp2(s).astype(BF16) for s in sc_all[b][c])
                    accs[c][...] += jnp.dot(vt_scr[j, :, 0:hk], p_top, preferred_element_type=F32)
                    accs[c][:, hk:tq] += jnp.dot(vt_scr[j, :, hk:tk], p_bot, preferred_element_type=F32)
                else:
                    p = jnp.exp2(sc_all[b][c]).astype(BF16)
                    accs[c][...] += jnp.dot(vt_scr[j], p, preferred_element_type=F32)

    def running_max_update(j, masked):
        vt = vt_scr[j]
        sc_j = scores(j, masked)
        for c in range(2):
            m_old = maxs[c][...]
            m_new = jnp.maximum(m_old, jnp.max(sc_j[c], axis=0, keepdims=True))
            p = jnp.exp2(sc_j[c] - m_new).astype(BF16)
            accs[c][...] = jnp.exp2(m_old - m_new) * accs[c][...] + jnp.dot(vt, p, preferred_element_type=F32)
            maxs[c][...] = m_new

    @pl.when(one_pass)
    def _():
        def pair(jj, carry):
            one_pass_update(((2 * jj, False), (2 * jj + 1, False)))
            return carry

        lax.fori_loop(0, i // 2, pair, 0)

        @pl.when(i % 2 == 1)
        def _():
            one_pass_update(((i - 1, False), (i, True)))

        @pl.when(i % 2 == 0)
        def _():
            one_pass_update(((i, True),))

    @pl.when(jnp.logical_not(one_pass))
    def _():
        def single(j, carry):
            running_max_update(j, False)
            return carry

        lax.fori_loop(0, i, single, 0)
        running_max_update(i, True)

    lv = lam_ref[...]
    lam = (jnp.exp(jnp.sum(lv[0:1, :] * lv[1:2, :], axis=-1, keepdims=True))
           - jnp.exp(jnp.sum(lv[2:3, :] * lv[3:4, :], axis=-1, keepdims=True)) + lam_init)
    o = (acc0_scr[0:DA_V, :] / acc0_scr[DA_V:DA_V + 1, :]
         - lam * (acc1_scr[0:DA_V, :] / acc1_scr[DA_V:DA_V + 1, :]))
    ms = jnp.mean(o * o, axis=0, keepdims=True)
    on = o * lax.rsqrt(ms + EPS) * sg_ref[...] * (1.0 - lam_init)
    o_ref[...] = on.astype(o_ref.dtype)


def _attn(proj, v_t, qg, kg, lam_vecs, sg, layer, batch, seq):
    n = proj.shape[0]
    tq = tk = min(512, seq)
    nqb = seq // tq
    lam_init = 0.8 - 0.6 * math.exp(-0.3 * layer)
    slopes = jnp.asarray(2.0 ** (-8.0 * jnp.arange(1, DA_HEADS + 1) / DA_HEADS), F32)
    slopes = jnp.broadcast_to(slopes[:, None, None], (DA_HEADS, 1, LANES))
    vec = lambda a: jnp.concatenate([a, a]).reshape(1, DA_V)
    sg_cols = jnp.broadcast_to(sg.astype(F32)[:, None], (DA_V, tq))
    v_rows = DA_V + V_PAD
    return pl.pallas_call(
        functools.partial(_attn_kernel, tq=tq, tk=tk, seq=seq, lam_init=lam_init),
        grid=(batch, DA_HEADS, nqb),
        in_specs=[
            pl.BlockSpec((None, 1, LANES), lambda b, h, i: (h, 0, 0)),
            pl.BlockSpec((tq, DA_V), lambda b, h, i: (b * nqb + i, COL_DQ // DA_V + h)),
            pl.BlockSpec((seq, DA_V), lambda b, h, i: (b, COL_DK // DA_V + h)),
            pl.BlockSpec((DA_V, seq), lambda b, h, i: (h, b)),
            pl.BlockSpec((1, DA_V), lambda b, h, i: (0, 0)),
            pl.BlockSpec((1, DA_V), lambda b, h, i: (0, 0)),
            pl.BlockSpec((4, DA_D), lambda b, h, i: (0, 0)),
            pl.BlockSpec((DA_V, tq), lambda b, h, i: (0, 0)),
        ],
        out_specs=pl.BlockSpec((DA_V, tq), lambda b, h, i: (h, b * nqb + i)),
        out_shape=jax.ShapeDtypeStruct((DA_WIDTH, n), BF16),
        scratch_shapes=[
            pltpu.VMEM((seq // tk, tk, 2 * LANES), BF16),
            pltpu.VMEM((seq // tk, v_rows, tk), BF16),
            pltpu.VMEM((tq, LANES), F32),
            pltpu.VMEM((2, tq, 2 * LANES), BF16),
            pltpu.VMEM((1, tq), F32), pltpu.VMEM((1, tq), F32),
            pltpu.VMEM((v_rows, tq), F32), pltpu.VMEM((v_rows, tq), F32),
        ],
        compiler_params=_cparams(("arbitrary", "arbitrary", "arbitrary")),
        name="diff_attn",
    )(slopes, proj, proj, v_t, vec(qg), vec(kg), lam_vecs, sg_cols)


def _merge_kernel(x_ref, ga0_ref, ga1_ref, gb0_ref, gb1_ref, ohg_ref, oda_ref, wpa_ref, wpb_ref, wo_ref,
                  mod_ref, gf_ref, xo_ref, h2_ref):
    a = jnp.dot(ohg_ref[...], wpa_ref[...], preferred_element_type=F32)
    b = lax.dot_general(oda_ref[...], wpb_ref[...], (((0,), (0,)), ((), ())),
                        preferred_element_type=F32)
    ga = jnp.concatenate([ga0_ref[...], ga1_ref[...]], axis=1).astype(F32)
    gb = jnp.concatenate([gb0_ref[...], gb1_ref[...]], axis=1).astype(F32)
    y = _sigmoid(ga) * a + _sigmoid(gb) * b
    mix = jnp.dot(y.astype(BF16), wo_ref[...], preferred_element_type=F32)
    xn = x_ref[...] + mod_ref[2:3, :] * mix
    xo_ref[...] = xn
    h = _rms(xn, gf_ref[...]) * (1.0 + mod_ref[4:5, :]) + mod_ref[3:4, :]
    h2_ref[...] = h.astype(h2_ref.dtype)


def _merge(x2, proj, o_hg, o_da, wpa, wpb, wo, mod, gain_ffn, seq, h2_dtype):
    n, d = x2.shape
    tm = min(512, seq)
    row = lambda w: pl.BlockSpec((tm, w), lambda i: (i, 0))
    full = lambda a: pl.BlockSpec(a.shape, lambda i: (0, 0))
    half = d // 2
    gate = lambda cb: pl.BlockSpec((tm, half), lambda i: (i, cb))
    return pl.pallas_call(
        _merge_kernel,
        grid=(n // tm,),
        in_specs=[
            row(d),
            gate(COL_GA // half), gate(COL_GA // half + 1),
            gate(COL_GB // half), gate(COL_GB // half + 1),
            row(HG_WIDTH), pl.BlockSpec((DA_WIDTH, tm), lambda i: (0, i)),
            full(wpa), full(wpb), full(wo),
            pl.BlockSpec((None, ADA_CHUNKS, d), lambda i: (i * tm // seq, 0, 0)),
            pl.BlockSpec((1, d), lambda i: (0, 0)),
        ],
        out_specs=[row(d), row(d)],
        out_shape=[jax.ShapeDtypeStruct((n, d), F32), jax.ShapeDtypeStruct((n, d), h2_dtype)],
        compiler_params=_cparams(("arbitrary",)),
        name="merge_outproj",
    )(x2, proj, proj, proj, proj, o_hg, o_da, wpa, wpb, wo, mod, gain_ffn.reshape(1, d))


def _ffn_kernel(h_ref, x_ref, mod_ref, w1_ref, w3_ref, w2_ref, o_ref, acc_scr):
    f = pl.program_id(1)

    @pl.when(f == 0)
    def _():
        acc_scr[...] = jnp.zeros_like(acc_scr)

    h = h_ref[...]
    g = _silu(jnp.dot(h, w1_ref[...], preferred_element_type=F32)) * jnp.dot(
        h, w3_ref[...], preferred_element_type=F32)
    acc_scr[...] += jnp.dot(g.astype(BF16), w2_ref[...], preferred_element_type=F32)

    @pl.when(f == pl.num_programs(1) - 1)
    def _():
        o_ref[...] = x_ref[...] + mod_ref[5:6, :] * acc_scr[...]


def _ffn(h2, x2, mod, w1, w3, w2, seq):
    n, d = x2.shape
    dff = w1.shape[1]
    tm = min(512, seq)
    tf = dff
    once = pl.Buffered(1)
    return pl.pallas_call(
        _ffn_kernel,
        grid=(n // tm, dff // tf),
        in_specs=[
            pl.BlockSpec((tm, d), lambda i, f: (i, 0)),
            pl.BlockSpec((tm, d), lambda i, f: (i, 0)),
            pl.BlockSpec((None, ADA_CHUNKS, d), lambda i, f: (i * tm // seq, 0, 0)),
            pl.BlockSpec((d, tf), lambda i, f: (0, f), pipeline_mode=once),
            pl.BlockSpec((d, tf), lambda i, f: (0, f), pipeline_mode=once),
            pl.BlockSpec((tf, d), lambda i, f: (f, 0), pipeline_mode=once),
        ],
        out_specs=pl.BlockSpec((tm, d), lambda i, f: (i, 0)),
        out_shape=jax.ShapeDtypeStruct((n, d), F32),
        scratch_shapes=[pltpu.VMEM((tm, d), F32)],
        compiler_params=_cparams(("arbitrary", "arbitrary")),
        name="dense_swiglu",
    )(h2, x2, mod, w1, w3, w2)


def _route_kernel(h_ref, r_ref, route_ref, cnt_ref, carry_scr):
    tm = h_ref.shape[0]

    @pl.when(pl.program_id(0) == 0)
    def _():
        carry_scr[...] = jnp.zeros_like(carry_scr)

    h = h_ref[...]
    h_hi = h.astype(BF16)
    h_lo = (h - h_hi.astype(F32)).astype(BF16)
    r = r_ref[...]
    r_hi = r.astype(BF16)
    r_lo = (r - r_hi.astype(F32)).astype(BF16)
    logits = (jnp.dot(h_hi, r_hi, preferred_element_type=F32) + jnp.dot(h_hi, r_lo, preferred_element_type=F32)
              + jnp.dot(h_lo, r_hi, preferred_element_type=F32))
    lane = lax.broadcasted_iota(jnp.int32, (tm, LANES), 1).astype(F32)
    lg = jnp.where(lane < N_EXPERTS, logits, -jnp.inf)
    m0 = jnp.max(lg, axis=-1, keepdims=True)
    i0 = jnp.min(jnp.where(lg == m0, lane, float(LANES)), axis=-1, keepdims=True)
    lg1 = jnp.where(lane == i0, -jnp.inf, lg)
    m1 = jnp.max(lg1, axis=-1, keepdims=True)
    i1 = jnp.min(jnp.where(lg1 == m1, lane, float(LANES)), axis=-1, keepdims=True)
    e = jnp.exp(m1 - m0)
    w0 = 1.0 / (1.0 + e)
    w1 = e / (1.0 + e)

    sel = jnp.where(lane == i0, 1.0, jnp.where(lane == i1, 1.0, 0.0))
    rr = lax.broadcasted_iota(jnp.int32, (tm, tm), 0)
    cc = lax.broadcasted_iota(jnp.int32, (tm, tm), 1)
    tri = jnp.where(rr > cc, 1.0, 0.0).astype(BF16)
    before = jnp.dot(tri, sel.astype(BF16), preferred_element_type=F32) + carry_scr[...]
    r0 = jnp.sum(jnp.where(lane == i0, before, 0.0), axis=-1, keepdims=True)
    r1 = jnp.sum(jnp.where(lane == i1, before, 0.0), axis=-1, keepdims=True)
    carry_scr[...] += jnp.sum(sel, axis=0, keepdims=True)

    route = jnp.where(lane == 0.0, i0, jnp.where(lane == 1.0, i1, jnp.where(
        lane == 2.0, r0, jnp.where(lane == 3.0, r1, jnp.where(
            lane == 4.0, w0, jnp.where(lane == 5.0, w1, 0.0))))))
    route_ref[...] = route
    cnt_ref[...] = carry_scr[...]


def _route(h2, router):
    n, d = h2.shape
    tm = min(512, n)
    r_pad = jnp.zeros((d, LANES), F32).at[:, :N_EXPERTS].set(router)
    return pl.pallas_call(
        _route_kernel,
        grid=(n // tm,),
        in_specs=[pl.BlockSpec((tm, d), lambda i: (i, 0)),
                  pl.BlockSpec((d, LANES), lambda i: (0, 0))],
        out_specs=[pl.BlockSpec((tm, LANES), lambda i: (i, 0)),
                   pl.BlockSpec((1, LANES), lambda i: (0, 0))],
        out_shape=[jax.ShapeDtypeStruct((n, LANES), F32), jax.ShapeDtypeStruct((1, LANES), F32)],
        scratch_shapes=[pltpu.VMEM((1, LANES), F32)],
        compiler_params=_cparams(("arbitrary",)),
        name="moe_route",
    )(h2, r_pad)


def _row_copy(src_hbm, dst_ref, src_row, dst_row, sem):
    return pltpu.make_async_copy(src_hbm.at[pl.ds(src_row, 1)], dst_ref.at[pl.ds(dst_row, 1)], sem)


def _row_dma_issue(n_tokens, start_token):
    def issue(g, c):
        for u in range(ROW_DMA_UNROLL):
            start_token(g * ROW_DMA_UNROLL + u)
        return c

    lax.fori_loop(0, n_tokens // ROW_DMA_UNROLL, issue, 0)


def _row_dma_drain(n_tokens, wait_row):
    def drain(g, c):
        for _ in range(2 * ROW_DMA_UNROLL):
            wait_row()
        return c

    lax.fori_loop(0, n_tokens // ROW_DMA_UNROLL, drain, 0)


def _dispatch_kernel(zt_ref, pos_ref, h_ref, xs_hbm, zero_scr, sem, zero_sem, *, tg, tm, n_tiles):
    @pl.when(pl.program_id(0) == 0)
    def _():
        zero_scr[...] = jnp.zeros_like(zero_scr)

        def zero_tile(t):
            return pltpu.make_async_copy(zero_scr, xs_hbm.at[pl.ds(pl.multiple_of(t * tm, tm), tm)], zero_sem)

        def start(t, c):
            @pl.when(zt_ref[t] == 1)
            def _():
                zero_tile(t).start()
            return c

        def wait(t, c):
            @pl.when(zt_ref[t] == 1)
            def _():
                zero_tile(t).wait()
            return c

        lax.fori_loop(0, n_tiles, start, 0)
        lax.fori_loop(0, n_tiles, wait, 0)

    def start_token(t):
        for k in range(2):
            _row_copy(h_ref, xs_hbm, t, pos_ref[2 * t + k], sem).start()

    _row_dma_issue(tg, start_token)
    _row_dma_drain(tg, lambda: _row_copy(h_ref, xs_hbm, 0, 0, sem).wait())


def _dispatch(h2, pos, zero_tiles, tm):
    n, d = h2.shape
    tg = min(512, n)
    n_tiles = zero_tiles.shape[0]
    return pl.pallas_call(
        functools.partial(_dispatch_kernel, tg=tg, tm=tm, n_tiles=n_tiles),
        grid=(n // tg,),
        in_specs=[pl.BlockSpec(memory_space=pltpu.SMEM),
                  pl.BlockSpec((2 * tg,), lambda i: (i,), memory_space=pltpu.SMEM),
                  pl.BlockSpec((tg, d), lambda i: (i, 0))],
        out_specs=pl.BlockSpec(memory_space=pl.ANY),
        out_shape=jax.ShapeDtypeStruct((n_tiles * tm, d), h2.dtype),
        scratch_shapes=[pltpu.VMEM((tm, d), h2.dtype), pltpu.SemaphoreType.DMA(()), pltpu.SemaphoreType.DMA(())],
        compiler_params=_cparams(("arbitrary",)),
        name="moe_dispatch",
    )(zero_tiles, pos, h2)


def _experts_kernel(te_ref, nu_ref, x_ref, w1_ref, w3_ref, w2_ref, y_ref, xb_scr, acc_scr):
    del te_ref
    i = pl.program_id(0)
    f = pl.program_id(1)
    used = i < nu_ref[0]

    @pl.when(jnp.logical_and(used, f == 0))
    def _():
        xb_scr[...] = x_ref[...].astype(BF16)
        acc_scr[...] = jnp.zeros_like(acc_scr)

    @pl.when(used)
    def _():
        xb = xb_scr[...]
        g = _silu(jnp.dot(xb, w1_ref[...], preferred_element_type=F32)) * jnp.dot(
            xb, w3_ref[...], preferred_element_type=F32)
        acc_scr[...] += jnp.dot(g.astype(BF16), w2_ref[...], preferred_element_type=F32)

    @pl.when(jnp.logical_and(used, f == pl.num_programs(1) - 1))
    def _():
        y_ref[...] = acc_scr[...]

    @pl.when(jnp.logical_and(jnp.logical_not(used), f == 0))
    def _():
        y_ref[...] = jnp.zeros_like(y_ref)


def _experts(xs, tile_expert, n_used, w1, w3, w2, tm):
    n_rows, d = xs.shape
    dff = w1.shape[2]
    tf = dff // 2
    nf = dff // tf
    n_tiles = n_rows // tm

    def wmap(which):
        def index_map(i, f, te, nu):
            ii = jnp.maximum(jnp.minimum(i, nu[0] - 1), 0)
            ff = jnp.where(i < nu[0], f, nf - 1)
            return (te[ii], 0, ff) if which == 0 else (te[ii], ff, 0)
        return index_map

    xmap = lambda i, f, te, nu: (jnp.maximum(jnp.minimum(i, nu[0] - 1), 0), 0)
    return pl.pallas_call(
        _experts_kernel,
        grid_spec=pltpu.PrefetchScalarGridSpec(
            num_scalar_prefetch=2,
            grid=(n_tiles, nf),
            in_specs=[
                pl.BlockSpec((tm, d), xmap),
                pl.BlockSpec((None, d, tf), wmap(0)),
                pl.BlockSpec((None, d, tf), wmap(0)),
                pl.BlockSpec((None, tf, d), wmap(1)),
            ],
            out_specs=pl.BlockSpec((tm, d), lambda i, f, te, nu: (i, 0)),
            scratch_shapes=[pltpu.VMEM((tm, d), BF16), pltpu.VMEM((tm, d), F32)],
        ),
        out_shape=jax.ShapeDtypeStruct((n_rows, d), F32),
        compiler_params=_cparams(("arbitrary", "arbitrary")),
        name="moe_experts",
    )(tile_expert, n_used, xs, w1, w3, w2)


def _combine_kernel(pos_first_ref, pos_next_ref, y_hbm, x_ref, route_ref, mod_ref, o_ref, ybuf, sems, *, tc):
    step = pl.program_id(0)
    slot = step % 2

    def fetch(pos_ref, to_slot):
        def start_token(t):
            for k in range(2):
                _row_copy(y_hbm, ybuf.at[to_slot, k], pos_ref[2 * t + k], t, sems.at[to_slot]).start()
        _row_dma_issue(tc, start_token)

    @pl.when(step == 0)
    def _():
        fetch(pos_first_ref, 0)

    @pl.when(step + 1 < pl.num_programs(0))
    def _():
        fetch(pos_next_ref, 1 - slot)

    _row_dma_drain(tc, lambda: _row_copy(y_hbm, ybuf.at[slot, 0], 0, 0, sems.at[slot]).wait())
    route = route_ref[...]
    f = route[:, 4:5] * ybuf[slot, 0] + route[:, 5:6] * ybuf[slot, 1]
    o_ref[...] = x_ref[...] + mod_ref[5:6, :] * f


def _combine(y, pos, x2, route, mod, seq):
    n, d = x2.shape
    tc = min(512, seq)
    last = n // tc - 1
    return pl.pallas_call(
        functools.partial(_combine_kernel, tc=tc),
        grid=(n // tc,),
        in_specs=[
            pl.BlockSpec((2 * tc,), lambda i: (0,), memory_space=pltpu.SMEM),
            pl.BlockSpec((2 * tc,), lambda i: (jnp.minimum(i + 1, last),), memory_space=pltpu.SMEM),
            pl.BlockSpec(memory_space=pl.ANY),
            pl.BlockSpec((tc, d), lambda i: (i, 0)),
            pl.BlockSpec((tc, LANES), lambda i: (i, 0)),
            pl.BlockSpec((None, ADA_CHUNKS, d), lambda i: (i * tc // seq, 0, 0)),
        ],
        out_specs=pl.BlockSpec((tc, d), lambda i: (i, 0)),
        out_shape=jax.ShapeDtypeStruct((n, d), F32),
        scratch_shapes=[pltpu.VMEM((2, 2, tc, d), F32), pltpu.SemaphoreType.DMA((2,))],
        compiler_params=_cparams(("arbitrary",)),
        name="moe_combine",
    )(pos, pos, y, x2, route, mod)


def _moe(h2, x2, mod, router, w1, w3, w2, seq):
    n, _ = x2.shape
    tm = min(512, n)
    route, counts = _route(h2, router)
    cnt = counts[0, :N_EXPERTS].astype(jnp.int32)
    padded = (cnt + tm - 1) // tm * tm
    ends = jnp.cumsum(padded)
    offs = ends - padded
    ids = route[:, 0:2].astype(jnp.int32)
    ranks = route[:, 2:4].astype(jnp.int32)
    pos = (offs[ids] + ranks).reshape(-1)
    n_tiles = 2 * n // tm + N_EXPERTS
    starts = jnp.arange(n_tiles, dtype=jnp.int32) * tm
    tile_expert = jnp.minimum(jnp.sum(starts[:, None] >= ends[None, :], axis=1), N_EXPERTS - 1).astype(jnp.int32)
    n_used = (ends[-1:] // tm).astype(jnp.int32)

    zero_tiles = jnp.logical_or(jnp.any(starts[:, None] + tm == ends[None, :], axis=1),
                                starts >= ends[-1]).astype(jnp.int32)
    xs = _dispatch(h2, pos, zero_tiles, tm)
    y = _experts(xs, tile_expert, n_used, w1, w3, w2, tm)
    return _combine(y, pos, x2, route, mod, seq)


def kernel(x, c, ada_w, ada_b, norm_mix_g, norm_ffn_g, w_in, hgrn_lb_logits, hgrn_norm_g, da_qnorm_g, da_knorm_g, da_lambda, da_subln_g, w_branch_a, w_branch_b, w_out, ffn_w1, ffn_w3, ffn_w2, moe_router, moe_w1, moe_w3, moe_w2):
    batch, seq, d = x.shape
    depth = ada_w.shape[0]
    n = batch * seq
    x2 = x.reshape(n, d)

    p = jax.nn.softmax(hgrn_lb_logits.astype(F32), axis=0)
    cum = jnp.cumsum(p, axis=0)
    lb_all = cum - cum[0:1]

    mods = _ada(c, ada_w, ada_b).reshape(depth, batch, ADA_CHUNKS, d)

    for l in range(depth):
        mod = mods[l]
        moe_layer = l % 2 == 1
        proj, v_t = _inproj(x2, mod, norm_mix_g[l], w_in[l].astype(BF16), da_qnorm_g[l], da_knorm_g[l], seq)
        o_hg = _hgrn(proj, lb_all[l], hgrn_norm_g[l], batch, seq)
        o_da = _attn(proj, v_t, da_qnorm_g[l], da_knorm_g[l], da_lambda[l], da_subln_g[l], l, batch, seq)
        x2, h2 = _merge(x2, proj, o_hg, o_da, w_branch_a[l].astype(BF16), w_branch_b[l].astype(BF16),
                        w_out[l].astype(BF16), mod, norm_ffn_g[l], seq, F32 if moe_layer else BF16)
        if moe_layer:
            x2 = _moe(h2, x2, mod, moe_router[l // 2], moe_w1[l // 2].astype(BF16),
                      moe_w3[l // 2].astype(BF16), moe_w2[l // 2].astype(BF16), seq)
        else:
            x2 = _ffn(h2, x2, mod, ffn_w1[l // 2].astype(BF16), ffn_w3[l // 2].astype(BF16),
                      ffn_w2[l // 2].astype(BF16), seq)
    return x2.reshape(batch, seq, d)
```

```python
import functools
import math

import jax
import jax.numpy as jnp
from jax import lax
from jax.experimental import pallas as pl
from jax.experimental.pallas import tpu as pltpu

F32 = jnp.float32
BF16 = jnp.bfloat16
EPS = 1e-6

D_MODEL = 1024
HG_HEADS = 4
HG_D = 128
HG_WIDTH = HG_HEADS * HG_D
HG_CHUNK = 16
DA_HEADS = 4
DA_D = 64
DA_V = 2 * DA_D
DA_WIDTH = DA_HEADS * DA_V
N_EXPERTS = 8
ADA_CHUNKS = 6
LANES = 128
COL_HQ, COL_HF, COL_HI, COL_HG = 0, 512, 1024, 1536
COL_DQ, COL_DK, COL_DV = 2048, 2560, 3072
COL_GA, COL_GB = 3584, 4608
D_IN = 5632
NEG_BIG = -1e30
LOG2E = 1.4426950408889634
V_PAD = 16
ROW_DMA_UNROLL = 8
BOUND_LIMIT = 80.0
VMEM_LIMIT = 56 * 1024 * 1024


def _cparams(sem):
    return pltpu.CompilerParams(dimension_semantics=sem, vmem_limit_bytes=VMEM_LIMIT)


def _sigmoid(x):
    return 1.0 / (1.0 + jnp.exp(-x))


def _silu(x):
    return x * _sigmoid(x)


def _rms(x, gain):
    ms = jnp.mean(x * x, axis=-1, keepdims=True)
    return x * lax.rsqrt(ms + EPS) * gain


def _ada_kernel(c_ref, w_ref, b_ref, o_ref):
    cs = _silu(c_ref[...])
    o_ref[...] = jnp.dot(cs, w_ref[...], precision=lax.Precision.HIGHEST,
                         preferred_element_type=F32) + b_ref[...]


def _ada(c, ada_w, ada_b):
    depth, d, n6 = ada_w.shape
    b = c.shape[0]
    tn = 1536
    return pl.pallas_call(
        _ada_kernel,
        grid=(depth, n6 // tn),
        in_specs=[
            pl.BlockSpec((b, d), lambda l, j: (0, 0)),
            pl.BlockSpec((None, d, tn), lambda l, j: (l, 0, j)),
            pl.BlockSpec((None, 1, tn), lambda l, j: (l, 0, j)),
        ],
        out_specs=pl.BlockSpec((None, b, tn), lambda l, j: (l, 0, j)),
        out_shape=jax.ShapeDtypeStruct((depth, b, n6), F32),
        compiler_params=_cparams(("arbitrary", "arbitrary")),
        name="ada_mod",
    )(c, ada_w, ada_b.reshape(depth, 1, n6))


def _norm_halves(x, g):
    lo = lax.broadcasted_iota(jnp.int32, (1, LANES), 1) < DA_D
    sq = x * x
    s_lo = jnp.sum(jnp.where(lo, sq, 0.0), axis=-1, keepdims=True)
    s_hi = jnp.sum(jnp.where(lo, 0.0, sq), axis=-1, keepdims=True)
    return x * lax.rsqrt(jnp.where(lo, s_lo, s_hi) * (1.0 / DA_D) + EPS) * g


def _inproj_kernel(x_ref, mod_ref, g_ref, w_ref, qg_ref, kg_ref, o_ref, vt_ref):
    h = (_rms(x_ref[...], g_ref[...]) * (1.0 + mod_ref[1:2, :]) + mod_ref[0:1, :]).astype(BF16)
    o_ref[:, 0:COL_DQ] = jnp.dot(h, w_ref[:, 0:COL_DQ], preferred_element_type=F32).astype(o_ref.dtype)
    att = jnp.dot(h, w_ref[:, COL_DQ:COL_GA], preferred_element_type=F32)
    for hd in range(DA_HEADS):
        c0 = hd * DA_V
        qn = _norm_halves(att[:, c0:c0 + DA_V], qg_ref[...]) * (DA_D ** -0.5 * LOG2E)
        kn = _norm_halves(att[:, DA_WIDTH + c0:DA_WIDTH + c0 + DA_V], kg_ref[...])
        o_ref[:, COL_DQ + c0:COL_DQ + c0 + DA_V] = qn.astype(o_ref.dtype)
        o_ref[:, COL_DK + c0:COL_DK + c0 + DA_V] = kn.astype(o_ref.dtype)
    v = att[:, 2 * DA_WIDTH:3 * DA_WIDTH]
    o_ref[:, COL_DV:COL_GA] = v.astype(o_ref.dtype)
    vt_ref[...] = v.T.astype(vt_ref.dtype)
    o_ref[:, COL_GA:D_IN] = jnp.dot(h, w_ref[:, COL_GA:D_IN], preferred_element_type=F32).astype(o_ref.dtype)


def _inproj(x2, mod, gain, w_in_bf, qg, kg, seq):
    n, d = x2.shape
    tm = min(512, seq)
    vec = lambda a: jnp.concatenate([a, a]).reshape(1, DA_V)
    return pl.pallas_call(
        _inproj_kernel,
        grid=(n // tm,),
        in_specs=[
            pl.BlockSpec((tm, d), lambda i: (i, 0)),
            pl.BlockSpec((None, ADA_CHUNKS, d), lambda i: (i * tm // seq, 0, 0)),
            pl.BlockSpec((1, d), lambda i: (0, 0)),
            pl.BlockSpec((d, D_IN), lambda i: (0, 0), pipeline_mode=pl.Buffered(1)),
            pl.BlockSpec((1, DA_V), lambda i: (0, 0)),
            pl.BlockSpec((1, DA_V), lambda i: (0, 0)),
        ],
        out_specs=[pl.BlockSpec((tm, D_IN), lambda i: (i, 0)),
                   pl.BlockSpec((DA_WIDTH, tm), lambda i: (0, i))],
        out_shape=[jax.ShapeDtypeStruct((n, D_IN), BF16),
                   jax.ShapeDtypeStruct((DA_WIDTH, n), BF16)],
        compiler_params=_cparams(("arbitrary",)),
        name="norm_inproj",
    )(x2, mod, gain.reshape(1, d), w_in_bf, vec(qg), vec(kg))


def _hgrn_kernel(q_ref, z_ref, i_ref, og_ref, lb_ref, g_ref, o_ref, st_scr, *, n_chunks):
    c_rows = HG_CHUNK

    @pl.when(pl.program_id(1) == 0)
    def _():
        st_scr[...] = jnp.zeros_like(st_scr)

    half = c_rows // 2
    row = lax.broadcasted_iota(jnp.int32, (c_rows, HG_D), 0)
    row_half = lax.broadcasted_iota(jnp.int32, (half, HG_D), 0)
    gain = g_ref[...]
    lb = lb_ref[...]
    log_lb = jnp.log(lb)
    log1m_lb = jnp.log1p(-lb)
    one_m_lb = 1.0 - lb

    def body(c, carry):
        r0 = pl.multiple_of(c * c_rows, c_rows)
        for h in range(HG_HEADS):
            cs = slice(h * HG_D, (h + 1) * HG_D)
            z = z_ref[pl.ds(r0, c_rows), cs].astype(F32)
            q = q_ref[pl.ds(r0, c_rows), cs].astype(F32)
            v = i_ref[pl.ds(r0, c_rows), cs].astype(F32)
            og = og_ref[pl.ds(r0, c_rows), cs].astype(F32)

            e = jnp.exp(-jnp.abs(z))
            log_sig = jnp.minimum(z, 0.0) - jnp.log(1.0 + e)
            y = log1m_lb[:, cs] + log_sig
            a = log_lb[:, cs]
            log_f = jnp.maximum(a, y) + jnp.log(1.0 + jnp.exp(-jnp.abs(a - y)))
            sig_neg = jnp.where(z >= 0.0, e, 1.0) / (1.0 + e)
            k_in = one_m_lb[:, cs] * sig_neg

            b = log_f * LOG2E
            for d in (1, 2, 4, 8):
                b = b + jnp.where(row >= d, pltpu.roll(b, d, 0), 0.0)
            b_last = b[c_rows - 1:c_rows, :]

            st = st_scr[h]
            qe = (q * jnp.exp2(b)).astype(BF16)
            o = lax.dot_general(qe, st.astype(BF16), (((1,), (1,)), ((), ())),
                                preferred_element_type=F32)
            o_half = [o[0:half, :], o[half:c_rows, :]]
            b_mid = b[half - 1:half, :]
            q_bot = q[half:c_rows, :] * jnp.exp2(b[half:c_rows, :] - b_mid)
            k_top = k_in[0:half, :] * jnp.exp2(b_mid - b[0:half, :])
            for s in range(c_rows):
                hi = s // half
                rel = jnp.where(row_half >= s - hi * half, b[hi * half:(hi + 1) * half, :] - b[s:s + 1, :], -jnp.inf)
                w = jnp.exp2(rel) * q[hi * half:(hi + 1) * half, :] * k_in[s:s + 1, :]
                o_half[hi] = o_half[hi] + jnp.sum(w, axis=-1, keepdims=True) * v[s:s + 1, :]
                if hi == 0:
                    w = q_bot * k_top[s:s + 1, :]
                    o_half[1] = o_half[1] + jnp.sum(w, axis=-1, keepdims=True) * v[s:s + 1, :]
            o = jnp.concatenate(o_half, axis=0)

            k_dec = k_in * jnp.exp2(b_last - b)
            u = lax.dot_general(v.astype(BF16), k_dec.astype(BF16), (((0,), (0,)), ((), ())),
                                preferred_element_type=F32)
            st_scr[h] = st * jnp.exp2(b_last) + u

            out = _rms(o, gain) * _silu(og)
            o_ref[pl.ds(r0, c_rows), cs] = out.astype(o_ref.dtype)
        return carry

    lax.fori_loop(0, n_chunks, body, 0, unroll=2)


def _hgrn(proj, lb, gain, batch, seq):
    n = proj.shape[0]
    ts = min(512, seq)
    nsb = seq // ts
    wb = HG_WIDTH

    def col(cb):
        return pl.BlockSpec((ts, wb), lambda b, i: (b * nsb + i, cb))

    return pl.pallas_call(
        functools.partial(_hgrn_kernel, n_chunks=ts // HG_CHUNK),
        grid=(batch, nsb),
        in_specs=[col(COL_HQ // wb), col(COL_HF // wb), col(COL_HI // wb), col(COL_HG // wb),
                  pl.BlockSpec((1, wb), lambda b, i: (0, 0)),
                  pl.BlockSpec((1, HG_D), lambda b, i: (0, 0))],
        out_specs=pl.BlockSpec((ts, wb), lambda b, i: (b * nsb + i, 0)),
        out_shape=jax.ShapeDtypeStruct((n, wb), BF16),
        scratch_shapes=[pltpu.VMEM((HG_HEADS, HG_D, HG_D), F32)],
        compiler_params=_cparams(("arbitrary", "arbitrary")),
        name="hgrn2_scan",
    )(proj, proj, proj, proj, lb.reshape(1, wb), gain.reshape(1, HG_D))


def _split3(x):
    a = x.astype(BF16).astype(F32)
    b = (x - a).astype(BF16).astype(F32)
    c = (x - a - b).astype(BF16).astype(F32)
    return a, b, c


def _attn_kernel(slope_ref, q_ref, k_ref, vt_ref, qg_ref, kg_ref, lam_ref, sg_ref, o_ref,
                 kn_scr, vt_scr, qaug_scr, qa_scr, m0_scr, m1_scr, acc0_scr, acc1_scr,
                 *, tq, tk, seq, lam_init):
    i = pl.program_id(2)
    lane = lax.broadcasted_iota(jnp.int32, (1, LANES), 1)
    lo = lane < DA_D
    slope2 = slope_ref[...] * LOG2E

    sa, sb, sc = _split3(slope2)

    def lane_table(shape, values):
        lane_idx = lax.broadcasted_iota(jnp.int32, shape, len(shape) - 1)
        out = jnp.zeros(shape, F32)
        for col, val in enumerate(values):
            out = jnp.where(lane_idx == col, val, out)
        return out

    @pl.when(i == 0)
    def _():
        rowk = lax.broadcasted_iota(jnp.int32, (tk, LANES), 0)
        rowq = lax.broadcasted_iota(jnp.int32, (tq, LANES), 0)
        ones_row = jnp.where(lax.broadcasted_iota(jnp.int32, (V_PAD, tk), 0) == 0, 1.0, 0.0).astype(BF16)
        key_hi = (rowk >> 1).astype(F32)
        key_lo = (rowk & 1).astype(F32)
        qry_hi = (rowq >> 1).astype(F32)
        qry_lo = (rowq & 1).astype(F32)
        qaug_scr[...] = lane_table((tq, LANES), [2.0 * sa, 2.0 * sb, 2.0 * sc, sa, sb, sc,
                                                 float(tk) * sa, float(tk) * sb, float(tk) * sc, 0.0, 0.0, 0.0,
                                                 qry_hi, qry_hi, qry_hi, qry_lo, qry_lo, qry_lo])

        key_cols = lane_table((tk, LANES), [key_hi, key_hi, key_hi, key_lo, key_lo, key_lo, 0.0, 0.0, 0.0, 1.0, 1.0, 1.0,
                                            -2.0 * sa, -2.0 * sb, -2.0 * sc, -sa, -sb, -sc])
        lanek = lax.broadcasted_iota(jnp.int32, (tk, LANES), 1)
        block_lanes = jnp.logical_and(lanek >= 6, lanek < 9)

        def kb(j, carry):
            r = pl.multiple_of(j * tk, tk)
            aug = jnp.where(block_lanes, lax.convert_element_type(j, F32), key_cols)
            kn_scr[j, :, 0:LANES] = k_ref[pl.ds(r, tk), :]
            kn_scr[j, :, LANES:2 * LANES] = aug.astype(BF16)
            vt_scr[j, 0:DA_V, :] = vt_ref[:, pl.ds(r, tk)]
            vt_scr[j, DA_V:DA_V + V_PAD, :] = ones_row
            return carry

        lax.fori_loop(0, seq // tk, kb, 0)

    qn = q_ref[...]
    gq = jnp.max(jnp.abs(qg_ref[...]), axis=-1, keepdims=True)
    gk = jnp.max(jnp.abs(kg_ref[...]), axis=-1, keepdims=True)
    bound = (1.02 * DA_D * DA_D ** -0.5 * LOG2E) * gq * gk
    one_pass = bound[0, 0] * 2.0 <= BOUND_LIMIT
    use_bound = jnp.where(bound * 2.0 <= BOUND_LIMIT, 1.0, 0.0)

    ca, cb, cc = _split3(-(bound * use_bound + slope2 * (i * tq).astype(F32)))
    block_cols = lane_table((1, LANES), [0.0] * 9 + [ca, cb, cc])
    lane_q = lax.broadcasted_iota(jnp.int32, (tq, LANES), 1)
    shift_cols = jnp.where(jnp.logical_and(lane_q >= 9, lane_q < 12), block_cols, qaug_scr[...]).astype(BF16)
    qa_scr[0, :, 0:LANES] = jnp.where(lo, qn, jnp.zeros_like(qn))
    qa_scr[1, :, 0:LANES] = jnp.where(lo, jnp.zeros_like(qn), qn)
    for c in range(2):
        qa_scr[c, :, LANES:2 * LANES] = shift_cols

    accs = (acc0_scr, acc1_scr)
    maxs = (m0_scr, m1_scr)
    for c in range(2):
        accs[c][...] = jnp.zeros_like(accs[c])
        maxs[c][...] = jnp.full_like(maxs[c], NEG_BIG)

    def scores(j, masked):
        kc = kn_scr[j]
        out = []
        for c in range(2):
            s = lax.dot_general(kc, qa_scr[c], (((1,), (1,)), ((), ())), preferred_element_type=F32)
            if masked:
                keyi = lax.broadcasted_iota(jnp.int32, (tk, tq), 0)
                qryi = lax.broadcasted_iota(jnp.int32, (tk, tq), 1)
                s = jnp.where(keyi <= qryi, s, NEG_BIG)
            out.append(s)
        return out

    hk = tk // 2

    def diag_scores(j):
        out = []
        for c in range(2):
            s_top = lax.dot_general(kn_scr[j, 0:hk, :], qa_scr[c], (((1,), (1,)), ((), ())),
                                    preferred_element_type=F32)
            s_bot = lax.dot_general(kn_scr[j, hk:tk, :], qa_scr[c, hk:tq, :], (((1,), (1,)), ((), ())),
                                    preferred_element_type=F32)
            tri = lambda s: jnp.where(lax.broadcasted_iota(jnp.int32, s.shape, 0)
                                      <= lax.broadcasted_iota(jnp.int32, s.shape, 1), s, NEG_BIG)
            out.append((tri(s_top), tri(s_bot)))
        return out

    def one_pass_update(blocks):
        sc_all = [diag_scores(j) if masked else scores(j, False) for j, masked in blocks]
        for b, (j, masked) in enumerate(blocks):
            for c in range(2):
                if masked:
                    p_top, p_bot = (jnp.exp2(s).astype(BF16) for s in sc_all[b][c])
                    accs[c][...] += jnp.dot(vt_scr[j, :, 0:hk], p_top, preferred_element_type=F32)
                    accs[c][:, hk:tq] += jnp.dot(vt_scr[j, :, hk:tk], p_bot, preferred_element_type=F32)
                else:
                    p = jnp.exp2(sc_all[b][c]).astype(BF16)
                    accs[c][...] += jnp.dot(vt_scr[j], p, preferred_element_type=F32)

    def running_max_update(j, masked):
        vt = vt_scr[j]
        sc_j = scores(j, masked)
        for c in range(2):
            m_old = maxs[c][...]
            m_new = jnp.maximum(m_old, jnp.max(sc_j[c], axis=0, keepdims=True))
            p = jnp.exp2(sc_j[c] - m_new).astype(BF16)
            accs[c][...] = jnp.exp2(m_old - m_new) * accs[c][...] + jnp.dot(vt, p, preferred_element_type=F32)
            maxs[c][...] = m_new

    @pl.when(one_pass)
    def _():
        def pair(jj, carry):
            one_pass_update(((2 * jj, False), (2 * jj + 1, False)))
            return carry

        lax.fori_loop(0, i // 2, pair, 0)

        @pl.when(i % 2 == 1)
        def _():
            one_pass_update(((i - 1, False), (i, True)))

        @pl.when(i % 2 == 0)
        def _():
            one_pass_update(((i, True),))

    @pl.when(jnp.logical_not(one_pass))
    def _():
        def single(j, carry):
            running_max_update(j, False)
            return carry

        lax.fori_loop(0, i, single, 0)
        running_max_update(i, True)

    lv = lam_ref[...]
    lam = (jnp.exp(jnp.sum(lv[0:1, :] * lv[1:2, :], axis=-1, keepdims=True))
           - jnp.exp(jnp.sum(lv[2:3, :] * lv[3:4, :], axis=-1, keepdims=True)) + lam_init)
    o = (acc0_scr[0:DA_V, :] / acc0_scr[DA_V:DA_V + 1, :]
         - lam * (acc1_scr[0:DA_V, :] / acc1_scr[DA_V:DA_V + 1, :]))
    ms = jnp.mean(o * o, axis=0, keepdims=True)
    on = o * lax.rsqrt(ms + EPS) * sg_ref[...] * (1.0 - lam_init)
    o_ref[...] = on.astype(o_ref.dtype)


def _attn(proj, v_t, qg, kg, lam_vecs, sg, layer, batch, seq):
    n = proj.shape[0]
    tq = tk = min(512, seq)
    nqb = seq // tq
    lam_init = 0.8 - 0.6 * math.exp(-0.3 * layer)
    slopes = jnp.asarray(2.0 ** (-8.0 * jnp.arange(1, DA_HEADS + 1) / DA_HEADS), F32)
    slopes = jnp.broadcast_to(slopes[:, None, None], (DA_HEADS, 1, LANES))
    vec = lambda a: jnp.concatenate([a, a]).reshape(1, DA_V)
    sg_cols = jnp.broadcast_to(sg.astype(F32)[:, None], (DA_V, tq))
    v_rows = DA_V + V_PAD
    return pl.pallas_call(
        functools.partial(_attn_kernel, tq=tq, tk=tk, seq=seq, lam_init=lam_init),
        grid=(batch, DA_HEADS, nqb),
        in_specs=[
            pl.BlockSpec((None, 1, LANES), lambda b, h, i: (h, 0, 0)),
            pl.BlockSpec((tq, DA_V), lambda b, h, i: (b * nqb + i, COL_DQ // DA_V + h)),
            pl.BlockSpec((seq, DA_V), lambda b, h, i: (b, COL_DK // DA_V + h)),
            pl.BlockSpec((DA_V, seq), lambda b, h, i: (h, b)),
            pl.BlockSpec((1, DA_V), lambda b, h, i: (0, 0)),
            pl.BlockSpec((1, DA_V), lambda b, h, i: (0, 0)),
            pl.BlockSpec((4, DA_D), lambda b, h, i: (0, 0)),
            pl.BlockSpec((DA_V, tq), lambda b, h, i: (0, 0)),
        ],
        out_specs=pl.BlockSpec((DA_V, tq), lambda b, h, i: (h, b * nqb + i)),
        out_shape=jax.ShapeDtypeStruct((DA_WIDTH, n), BF16),
        scratch_shapes=[
            pltpu.VMEM((seq // tk, tk, 2 * LANES), BF16),
            pltpu.VMEM((seq // tk, v_rows, tk), BF16),
            pltpu.VMEM((tq, LANES), F32),
            pltpu.VMEM((2, tq, 2 * LANES), BF16),
            pltpu.VMEM((1, tq), F32), pltpu.VMEM((1, tq), F32),
            pltpu.VMEM((v_rows, tq), F32), pltpu.VMEM((v_rows, tq), F32),
        ],
        compiler_params=_cparams(("arbitrary", "arbitrary", "arbitrary")),
        name="diff_attn",
    )(slopes, proj, proj, v_t, vec(qg), vec(kg), lam_vecs, sg_cols)


def _merge_kernel(x_ref, ga0_ref, ga1_ref, gb0_ref, gb1_ref, ohg_ref, oda_ref, wpa_ref, wpb_ref, wo_ref,
                  mod_ref, gf_ref, xo_ref, h2_ref):
    a = jnp.dot(ohg_ref[...], wpa_ref[...], preferred_element_type=F32)
    b = lax.dot_general(oda_ref[...], wpb_ref[...], (((0,), (0,)), ((), ())),
                        preferred_element_type=F32)
    ga = jnp.concatenate([ga0_ref[...], ga1_ref[...]], axis=1).astype(F32)
    gb = jnp.concatenate([gb0_ref[...], gb1_ref[...]], axis=1).astype(F32)
    y = _sigmoid(ga) * a + _sigmoid(gb) * b
    mix = jnp.dot(y.astype(BF16), wo_ref[...], preferred_element_type=F32)
    xn = x_ref[...] + mod_ref[2:3, :] * mix
    xo_ref[...] = xn
    h = _rms(xn, gf_ref[...]) * (1.0 + mod_ref[4:5, :]) + mod_ref[3:4, :]
    h2_ref[...] = h.astype(h2_ref.dtype)


def _merge(x2, proj, o_hg, o_da, wpa, wpb, wo, mod, gain_ffn, seq, h2_dtype):
    n, d = x2.shape
    tm = min(512, seq)
    row = lambda w: pl.BlockSpec((tm, w), lambda i: (i, 0))
    full = lambda a: pl.BlockSpec(a.shape, lambda i: (0, 0))
    half = d // 2
    gate = lambda cb: pl.BlockSpec((tm, half), lambda i: (i, cb))
    return pl.pallas_call(
        _merge_kernel,
        grid=(n // tm,),
        in_specs=[
            row(d),
            gate(COL_GA // half), gate(COL_GA // half + 1),
            gate(COL_GB // half), gate(COL_GB // half + 1),
            row(HG_WIDTH), pl.BlockSpec((DA_WIDTH, tm), lambda i: (0, i)),
            full(wpa), full(wpb), full(wo),
            pl.BlockSpec((None, ADA_CHUNKS, d), lambda i: (i * tm // seq, 0, 0)),
            pl.BlockSpec((1, d), lambda i: (0, 0)),
        ],
        out_specs=[row(d), row(d)],
        out_shape=[jax.ShapeDtypeStruct((n, d), F32), jax.ShapeDtypeStruct((n, d), h2_dtype)],
        compiler_params=_cparams(("arbitrary",)),
        name="merge_outproj",
    )(x2, proj, proj, proj, proj, o_hg, o_da, wpa, wpb, wo, mod, gain_ffn.reshape(1, d))


def _ffn_kernel(h_ref, x_ref, mod_ref, w1_ref, w3_ref, w2_ref, o_ref, acc_scr):
    f = pl.program_id(1)

    @pl.when(f == 0)
    def _():
        acc_scr[...] = jnp.zeros_like(acc_scr)

    h = h_ref[...]
    g = _silu(jnp.dot(h, w1_ref[...], preferred_element_type=F32)) * jnp.dot(
        h, w3_ref[...], preferred_element_type=F32)
    acc_scr[...] += jnp.dot(g.astype(BF16), w2_ref[...], preferred_element_type=F32)

    @pl.when(f == pl.num_programs(1) - 1)
    def _():
        o_ref[...] = x_ref[...] + mod_ref[5:6, :] * acc_scr[...]


def _ffn(h2, x2, mod, w1, w3, w2, seq):
    n, d = x2.shape
    dff = w1.shape[1]
    tm = min(512, seq)
    tf = dff
    once = pl.Buffered(1)
    return pl.pallas_call(
        _ffn_kernel,
        grid=(n // tm, dff // tf),
        in_specs=[
            pl.BlockSpec((tm, d), lambda i, f: (i, 0)),
            pl.BlockSpec((tm, d), lambda i, f: (i, 0)),
            pl.BlockSpec((None, ADA_CHUNKS, d), lambda i, f: (i * tm // seq, 0, 0)),
            pl.BlockSpec((d, tf), lambda i, f: (0, f), pipeline_mode=once),
            pl.BlockSpec((d, tf), lambda i, f: (0, f), pipeline_mode=once),
            pl.BlockSpec((tf, d), lambda i, f: (f, 0), pipeline_mode=once),
        ],
        out_specs=pl.BlockSpec((tm, d), lambda i, f: (i, 0)),
        out_shape=jax.ShapeDtypeStruct((n, d), F32),
        scratch_shapes=[pltpu.VMEM((tm, d), F32)],
        compiler_params=_cparams(("arbitrary", "arbitrary")),
        name="dense_swiglu",
    )(h2, x2, mod, w1, w3, w2)


def _route_kernel(h_ref, r_ref, route_ref, cnt_ref, carry_scr):
    tm = h_ref.shape[0]

    @pl.when(pl.program_id(0) == 0)
    def _():
        carry_scr[...] = jnp.zeros_like(carry_scr)

    h = h_ref[...]
    h_hi = h.astype(BF16)
    h_lo = (h - h_hi.astype(F32)).astype(BF16)
    r = r_ref[...]
    r_hi = r.astype(BF16)
    r_lo = (r - r_hi.astype(F32)).astype(BF16)
    logits = (jnp.dot(h_hi, r_hi, preferred_element_type=F32) + jnp.dot(h_hi, r_lo, preferred_element_type=F32)
              + jnp.dot(h_lo, r_hi, preferred_element_type=F32))
    lane = lax.broadcasted_iota(jnp.int32, (tm, LANES), 1).astype(F32)
    lg = jnp.where(lane < N_EXPERTS, logits, -jnp.inf)
    m0 = jnp.max(lg, axis=-1, keepdims=True)
    i0 = jnp.min(jnp.where(lg == m0, lane, float(LANES)), axis=-1, keepdims=True)
    lg1 = jnp.where(lane == i0, -jnp.inf, lg)
    m1 = jnp.max(lg1, axis=-1, keepdims=True)
    i1 = jnp.min(jnp.where(lg1 == m1, lane, float(LANES)), axis=-1, keepdims=True)
    e = jnp.exp(m1 - m0)
    w0 = 1.0 / (1.0 + e)
    w1 = e / (1.0 + e)

    sel = jnp.where(lane == i0, 1.0, jnp.where(lane == i1, 1.0, 0.0))
    rr = lax.broadcasted_iota(jnp.int32, (tm, tm), 0)
    cc = lax.broadcasted_iota(jnp.int32, (tm, tm), 1)
    tri = jnp.where(rr > cc, 1.0, 0.0).astype(BF16)
    before = jnp.dot(tri, sel.astype(BF16), preferred_element_type=F32) + carry_scr[...]
    r0 = jnp.sum(jnp.where(lane == i0, before, 0.0), axis=-1, keepdims=True)
    r1 = jnp.sum(jnp.where(lane == i1, before, 0.0), axis=-1, keepdims=True)
    carry_scr[...] += jnp.sum(sel, axis=0, keepdims=True)

    route = jnp.where(lane == 0.0, i0, jnp.where(lane == 1.0, i1, jnp.where(
        lane == 2.0, r0, jnp.where(lane == 3.0, r1, jnp.where(
            lane == 4.0, w0, jnp.where(lane == 5.0, w1, 0.0))))))
    route_ref[...] = route
    cnt_ref[...] = carry_scr[...]


def _route(h2, router):
    n, d = h2.shape
    tm = min(512, n)
    r_pad = jnp.zeros((d, LANES), F32).at[:, :N_EXPERTS].set(router)
    return pl.pallas_call(
        _route_kernel,
        grid=(n // tm,),
        in_specs=[pl.BlockSpec((tm, d), lambda i: (i, 0)),
                  pl.BlockSpec((d, LANES), lambda i: (0, 0))],
        out_specs=[pl.BlockSpec((tm, LANES), lambda i: (i, 0)),
                   pl.BlockSpec((1, LANES), lambda i: (0, 0))],
        out_shape=[jax.ShapeDtypeStruct((n, LANES), F32), jax.ShapeDtypeStruct((1, LANES), F32)],
        scratch_shapes=[pltpu.VMEM((1, LANES), F32)],
        compiler_params=_cparams(("arbitrary",)),
        name="moe_route",
    )(h2, r_pad)


def _row_copy(src_hbm, dst_ref, src_row, dst_row, sem):
    return pltpu.make_async_copy(src_hbm.at[pl.ds(src_row, 1)], dst_ref.at[pl.ds(dst_row, 1)], sem)


def _row_dma_issue(n_tokens, start_token):
    def issue(g, c):
        for u in range(ROW_DMA_UNROLL):
            start_token(g * ROW_DMA_UNROLL + u)
        return c

    lax.fori_loop(0, n_tokens // ROW_DMA_UNROLL, issue, 0)


def _row_dma_drain(n_tokens, wait_row):
    def drain(g, c):
        for _ in range(2 * ROW_DMA_UNROLL):
            wait_row()
        return c

    lax.fori_loop(0, n_tokens // ROW_DMA_UNROLL, drain, 0)


def _dispatch_kernel(zt_ref, pos_ref, h_ref, xs_hbm, zero_scr, sem, zero_sem, *, tg, tm, n_tiles):
    @pl.when(pl.program_id(0) == 0)
    def _():
        zero_scr[...] = jnp.zeros_like(zero_scr)

        def zero_tile(t):
            return pltpu.make_async_copy(zero_scr, xs_hbm.at[pl.ds(pl.multiple_of(t * tm, tm), tm)], zero_sem)

        def start(t, c):
            @pl.when(zt_ref[t] == 1)
            def _():
                zero_tile(t).start()
            return c

        def wait(t, c):
            @pl.when(zt_ref[t] == 1)
            def _():
                zero_tile(t).wait()
            return c

        lax.fori_loop(0, n_tiles, start, 0)
        lax.fori_loop(0, n_tiles, wait, 0)

    def start_token(t):
        for k in range(2):
            _row_copy(h_ref, xs_hbm, t, pos_ref[2 * t + k], sem).start(priority=k)

    _row_dma_issue(tg, start_token)
    _row_dma_drain(tg, lambda: _row_copy(h_ref, xs_hbm, 0, 0, sem).wait())


def _dispatch(h2, pos, zero_tiles, tm):
    n, d = h2.shape
    tg = min(512, n)
    n_tiles = zero_tiles.shape[0]
    return pl.pallas_call(
        functools.partial(_dispatch_kernel, tg=tg, tm=tm, n_tiles=n_tiles),
        grid=(n // tg,),
        in_specs=[pl.BlockSpec(memory_space=pltpu.SMEM),
                  pl.BlockSpec((2 * tg,), lambda i: (i,), memory_space=pltpu.SMEM),
                  pl.BlockSpec((tg, d), lambda i: (i, 0))],
        out_specs=pl.BlockSpec(memory_space=pl.ANY),
        out_shape=jax.ShapeDtypeStruct((n_tiles * tm, d), h2.dtype),
        scratch_shapes=[pltpu.VMEM((tm, d), h2.dtype), pltpu.SemaphoreType.DMA(()), pltpu.SemaphoreType.DMA(())],
        compiler_params=_cparams(("arbitrary",)),
        name="moe_dispatch",
    )(zero_tiles, pos, h2)


def _experts_kernel(te_ref, nu_ref, x_ref, w1_ref, w3_ref, w2_ref, y_ref, xb_scr, acc_scr):
    del te_ref
    i = pl.program_id(0)
    f = pl.program_id(1)
    used = i < nu_ref[0]

    @pl.when(jnp.logical_and(used, f == 0))
    def _():
        xb_scr[...] = x_ref[...].astype(BF16)
        acc_scr[...] = jnp.zeros_like(acc_scr)

    @pl.when(used)
    def _():
        xb = xb_scr[...]
        g = _silu(jnp.dot(xb, w1_ref[...], preferred_element_type=F32)) * jnp.dot(
            xb, w3_ref[...], preferred_element_type=F32)
        acc_scr[...] += jnp.dot(g.astype(BF16), w2_ref[...], preferred_element_type=F32)

    @pl.when(jnp.logical_and(used, f == pl.num_programs(1) - 1))
    def _():
        y_ref[...] = acc_scr[...]

    @pl.when(jnp.logical_and(jnp.logical_not(used), f == 0))
    def _():
        y_ref[...] = jnp.zeros_like(y_ref)


def _experts(xs, tile_expert, n_used, w1, w3, w2, tm):
    n_rows, d = xs.shape
    dff = w1.shape[2]
    tf = dff // 2
    nf = dff // tf
    n_tiles = n_rows // tm

    def wmap(which):
        def index_map(i, f, te, nu):
            ii = jnp.maximum(jnp.minimum(i, nu[0] - 1), 0)
            ff = jnp.where(i < nu[0], f, nf - 1)
            return (te[ii], 0, ff) if which == 0 else (te[ii], ff, 0)
        return index_map

    xmap = lambda i, f, te, nu: (jnp.maximum(jnp.minimum(i, nu[0] - 1), 0), 0)
    return pl.pallas_call(
        _experts_kernel,
        grid_spec=pltpu.PrefetchScalarGridSpec(
            num_scalar_prefetch=2,
            grid=(n_tiles, nf),
            in_specs=[
                pl.BlockSpec((tm, d), xmap),
                pl.BlockSpec((None, d, tf), wmap(0)),
                pl.BlockSpec((None, d, tf), wmap(0)),
                pl.BlockSpec((None, tf, d), wmap(1)),
            ],
            out_specs=pl.BlockSpec((tm, d), lambda i, f, te, nu: (i, 0)),
            scratch_shapes=[pltpu.VMEM((tm, d), BF16), pltpu.VMEM((tm, d), F32)],
        ),
        out_shape=jax.ShapeDtypeStruct((n_rows, d), F32),
        compiler_params=_cparams(("arbitrary", "arbitrary")),
        name="moe_experts",
    )(tile_expert, n_used, xs, w1, w3, w2)


def _combine_kernel(pos_first_ref, pos_next_ref, y_hbm, x_ref, route_ref, mod_ref, o_ref, ybuf, sems, *, tc):
    step = pl.program_id(0)
    slot = step % 2

    def fetch(pos_ref, to_slot):
        def start_token(t):
            for k in range(2):
                _row_copy(y_hbm, ybuf.at[to_slot, k], pos_ref[2 * t + k], t, sems.at[to_slot]).start(priority=k)
        _row_dma_issue(tc, start_token)

    @pl.when(step == 0)
    def _():
        fetch(pos_first_ref, 0)

    @pl.when(step + 1 < pl.num_programs(0))
    def _():
        fetch(pos_next_ref, 1 - slot)

    _row_dma_drain(tc, lambda: _row_copy(y_hbm, ybuf.at[slot, 0], 0, 0, sems.at[slot]).wait())
    route = route_ref[...]
    f = route[:, 4:5] * ybuf[slot, 0] + route[:, 5:6] * ybuf[slot, 1]
    o_ref[...] = x_ref[...] + mod_ref[5:6, :] * f


def _combine(y, pos, x2, route, mod, seq):
    n, d = x2.shape
    tc = min(512, seq)
    last = n // tc - 1
    return pl.pallas_call(
        functools.partial(_combine_kernel, tc=tc),
        grid=(n // tc,),
        in_specs=[
            pl.BlockSpec((2 * tc,), lambda i: (0,), memory_space=pltpu.SMEM),
            pl.BlockSpec((2 * tc,), lambda i: (jnp.minimum(i + 1, last),), memory_space=pltpu.SMEM),
            pl.BlockSpec(memory_space=pl.ANY),
            pl.BlockSpec((tc, d), lambda i: (i, 0)),
            pl.BlockSpec((tc, LANES), lambda i: (i, 0)),
            pl.BlockSpec((None, ADA_CHUNKS, d), lambda i: (i * tc // seq, 0, 0)),
        ],
        out_specs=pl.BlockSpec((tc, d), lambda i: (i, 0)),
        out_shape=jax.ShapeDtypeStruct((n, d), F32),
        scratch_shapes=[pltpu.VMEM((2, 2, tc, d), F32), pltpu.SemaphoreType.DMA((2,))],
        compiler_params=_cparams(("arbitrary",)),
        name="moe_combine",
    )(pos, pos, y, x2, route, mod)


def _moe(h2, x2, mod, router, w1, w3, w2, seq):
    n, _ = x2.shape
    tm = min(512, n)
    route, counts = _route(h2, router)
    cnt = counts[0, :N_EXPERTS].astype(jnp.int32)
    padded = (cnt + tm - 1) // tm * tm
    ends = jnp.cumsum(padded)
    offs = ends - padded
    ids = route[:, 0:2].astype(jnp.int32)
    ranks = route[:, 2:4].astype(jnp.int32)
    pos = (offs[ids] + ranks).reshape(-1)
    n_tiles = 2 * n // tm + N_EXPERTS
    starts = jnp.arange(n_tiles, dtype=jnp.int32) * tm
    tile_expert = jnp.minimum(jnp.sum(starts[:, None] >= ends[None, :], axis=1), N_EXPERTS - 1).astype(jnp.int32)
    n_used = (ends[-1:] // tm).astype(jnp.int32)

    zero_tiles = jnp.logical_or(jnp.any(starts[:, None] + tm == ends[None, :], axis=1),
                                starts >= ends[-1]).astype(jnp.int32)
    xs = _dispatch(h2, pos, zero_tiles, tm)
    y = _experts(xs, tile_expert, n_used, w1, w3, w2, tm)
    return _combine(y, pos, x2, route, mod, seq)


def kernel(x, c, ada_w, ada_b, norm_mix_g, norm_ffn_g, w_in, hgrn_lb_logits, hgrn_norm_g, da_qnorm_g, da_knorm_g, da_lambda, da_subln_g, w_branch_a, w_branch_b, w_out, ffn_w1, ffn_w3, ffn_w2, moe_router, moe_w1, moe_w3, moe_w2):
    batch, seq, d = x.shape
    depth = ada_w.shape[0]
    n = batch * seq
    x2 = x.reshape(n, d)

    p = jax.nn.softmax(hgrn_lb_logits.astype(F32), axis=0)
    cum = jnp.cumsum(p, axis=0)
    lb_all = cum - cum[0:1]

    mods = _ada(c, ada_w, ada_b).reshape(depth, batch, ADA_CHUNKS, d)

    for l in range(depth):
        mod = mods[l]
        moe_layer = l % 2 == 1
        proj, v_t = _inproj(x2, mod, norm_mix_g[l], w_in[l].astype(BF16), da_qnorm_g[l], da_knorm_g[l], seq)
        o_hg = _hgrn(proj, lb_all[l], hgrn_norm_g[l], batch, seq)
        o_da = _attn(proj, v_t, da_qnorm_g[l], da_knorm_g[l], da_lambda[l], da_subln_g[l], l, batch, seq)
        x2, h2 = _merge(x2, proj, o_hg, o_da, w_branch_a[l].astype(BF16), w_branch_b[l].astype(BF16),
                        w_out[l].astype(BF16), mod, norm_ffn_g[l], seq, F32 if moe_layer else BF16)
        if moe_layer:
            x2 = _moe(h2, x2, mod, moe_router[l // 2], moe_w1[l // 2].astype(BF16),
                      moe_w3[l // 2].astype(BF16), moe_w2[l // 2].astype(BF16), seq)
        else:
            x2 = _ffn(h2, x2, mod, ffn_w1[l // 2].astype(BF16), ffn_w3[l // 2].astype(BF16),
                      ffn_w2[l // 2].astype(BF16), seq)
    return x2.reshape(batch, seq, d)
```
